```python
import math
import jax, jax.numpy as jnp
from jax import lax
import numpy as np

D_MODEL = 2048
BATCH = 8
SEQ = 2048
DEPTH = 2

CHUNK = 64
N_MIXERS = 2
Q_BLOCK = 128
SB_HEADS = 16
SB_HEAD_DIM = D_MODEL // SB_HEADS
SSD_EXPAND = 2
SSD_D_INNER = SSD_EXPAND * D_MODEL
SSD_HEAD_DIM = 64
SSD_HEADS = SSD_D_INNER // SSD_HEAD_DIM
SSD_GROUPS = 8
SSD_HEADS_PER_GROUP = SSD_HEADS // SSD_GROUPS
SSD_STATE = 128
SSD_CONV = 4
SSD_CONV_DIM = SSD_D_INNER + 2 * SSD_GROUPS * SSD_STATE
SSD_IN_DIM = SSD_D_INNER + SSD_CONV_DIM + SSD_HEADS
D_FF = 5632
N_SB_LAYERS = (DEPTH + 1) // 2
N_SSD_LAYERS = DEPTH // 2
DEEPNORM_ALPHA = (2.0 * DEPTH) ** 0.25
DEEPNORM_BETA = (8.0 * DEPTH) ** -0.25
LN_EPS = 1e-5
RMS_EPS = 1e-5

kernel_name = "hybrid_stickbreak_ssd_macaron_deepnorm"


def layer_norm(x, g, b):
    xf = x.astype(jnp.float32)
    mu = jnp.mean(xf, axis=-1, keepdims=True)
    var = jnp.mean(jnp.square(xf - mu), axis=-1, keepdims=True)
    return ((xf - mu) * lax.rsqrt(var + LN_EPS) * g + b).astype(x.dtype)


def swiglu(x, w_gate_up, w_down):
    gate, up = jnp.split(x @ w_gate_up, 2, axis=-1)
    return (jax.nn.silu(gate) * up) @ w_down


def stick_breaking_attention(x, w_in, w_out):
    b, s, _ = x.shape
    qkv = (x @ w_in).reshape(b, s, 3, SB_HEADS, SB_HEAD_DIM)
    q = jnp.moveaxis(qkv[:, :, 0], 1, 2)
    k = jnp.moveaxis(qkv[:, :, 1], 1, 2)
    v = jnp.moveaxis(qkv[:, :, 2], 1, 2)
    n_blk = s // Q_BLOCK
    q_blocks = jnp.moveaxis(q.reshape(b, SB_HEADS, n_blk, Q_BLOCK, SB_HEAD_DIM), 2, 0)
    k_pos = jnp.arange(s)
    scale = SB_HEAD_DIM ** -0.5

    def one_block(args):
        blk, qb = args
        z = jnp.einsum('bhqd,bhkd->bhqk', qb, k).astype(jnp.float32) * scale
        q_pos = blk * Q_BLOCK + jnp.arange(Q_BLOCK)
        strict = k_pos[None, :] < q_pos[:, None]
        log_keep = jnp.where(strict, jax.nn.log_sigmoid(-z), 0.0)
        log_rest = lax.cumsum(log_keep, axis=3, reverse=True) - log_keep
        att = jnp.where(strict, jnp.exp(jax.nn.log_sigmoid(z) + log_rest), 0.0)
        return jnp.einsum('bhqk,bhkd->bhqd', att.astype(v.dtype), v)

    o = lax.map(one_block, (jnp.arange(n_blk), q_blocks))
    o = jnp.moveaxis(o, 0, 2).reshape(b, SB_HEADS, s, SB_HEAD_DIM)
    o = jnp.moveaxis(o, 1, 2).reshape(b, s, D_MODEL)
    return o @ w_out


def ssd_mixer(x, w_in, conv_w, conv_b, dt_bias, a_log, d_skip, norm_g, w_out):
    f32 = jnp.float32
    b, s, _ = x.shape
    nc = s // CHUNK
    G, R, P, N = SSD_GROUPS, SSD_HEADS_PER_GROUP, SSD_HEAD_DIM, SSD_STATE
    proj = x @ w_in
    z, xbc, dt_raw = jnp.split(proj, [SSD_D_INNER, SSD_D_INNER + SSD_CONV_DIM], axis=-1)
    xbc = lax.conv_general_dilated(
        xbc, conv_w[:, None, :], window_strides=(1,), padding=[(SSD_CONV - 1, 0)],
        dimension_numbers=('NWC', 'WIO', 'NWC'), feature_group_count=SSD_CONV_DIM) + conv_b
    xbc = jax.nn.silu(xbc)
    xs, bm, cm = jnp.split(xbc, [SSD_D_INNER, SSD_D_INNER + G * N], axis=-1)
    xs = xs.astype(f32).reshape(b, nc, CHUNK, G, R, P)
    bm = bm.astype(f32).reshape(b, nc, CHUNK, G, N)
    cm = cm.astype(f32).reshape(b, nc, CHUNK, G, N)
    dt = jax.nn.softplus(dt_raw.astype(f32) + dt_bias).reshape(b, nc, CHUNK, G, R)
    a = -jnp.exp(a_log.astype(f32)).reshape(G, R)
    a_cum = jnp.cumsum(dt * a, axis=2)

    causal = jnp.tril(jnp.ones((CHUNK, CHUNK), dtype=bool))[:, :, None, None]
    seg = a_cum[:, :, :, None] - a_cum[:, :, None, :]
    decay = jnp.exp(jnp.where(causal, seg, -jnp.inf))
    cb = jnp.einsum('bcign,bcjgn->bcijg', cm, bm)
    y_diag = jnp.einsum('bcijgr,bcjgrp->bcigrp', cb[..., None] * decay * dt[:, :, None], xs)

    decay_to_end = jnp.exp(a_cum[:, :, -1:] - a_cum)
    states = jnp.einsum('bcjgn,bcjgr,bcjgrp->bcgrpn', bm, decay_to_end * dt, xs)
    chunk_decay = jnp.exp(a_cum[:, :, -1])

    def step(h, inp):
        dec, st = inp
        return dec[..., None, None] * h + st, h

    h0 = jnp.zeros((b, G, R, P, N), f32)
    _, prev = lax.scan(step, h0, (jnp.moveaxis(chunk_decay, 1, 0), jnp.moveaxis(states, 1, 0)))
    prev = jnp.moveaxis(prev, 0, 1)
    y_off = jnp.einsum('bcign,bcgrpn,bcigr->bcigrp', cm, prev, jnp.exp(a_cum))

    y = y_diag + y_off + xs * d_skip.astype(f32).reshape(G, R)[:, :, None]
    y = y.reshape(b, s, SSD_D_INNER) * jax.nn.silu(z.astype(f32))
    yg = y.reshape(b, s, G, SSD_D_INNER // G)
    yg = yg * lax.rsqrt(jnp.mean(jnp.square(yg), axis=-1, keepdims=True) + RMS_EPS)
    y = yg.reshape(b, s, SSD_D_INNER) * norm_g
    return y.astype(x.dtype) @ w_out


def setup_inputs(seed: int = 0) -> dict:
    key = jax.random.key(seed)
    ks = jax.random.split(key, 20)
    f32 = jnp.float32

    def nrm(k, shape, fan_in, gain=1.0):
        return jax.random.normal(k, shape, f32) * (gain * fan_in ** -0.5)

    x = jax.random.normal(ks[0], (BATCH, SEQ, D_MODEL), f32)
    ffn_w_gate_up = nrm(ks[1], (DEPTH, 2, D_MODEL, 2 * D_FF), D_MODEL, DEEPNORM_BETA)
    ffn_w_down = nrm(ks[2], (DEPTH, 2, D_FF, D_MODEL), D_FF, DEEPNORM_BETA)
    ln_g = 1.0 + 0.02 * jax.random.normal(ks[3], (DEPTH, 3, D_MODEL), f32)
    ln_b = 0.02 * jax.random.normal(ks[4], (DEPTH, 3, D_MODEL), f32)
    sb_cols = jnp.arange(3 * D_MODEL)
    sb_scale = jnp.where(sb_cols >= 2 * D_MODEL, DEEPNORM_BETA, 1.0).astype(f32)
    sb_w_in = nrm(ks[5], (N_SB_LAYERS, D_MODEL, 3 * D_MODEL), D_MODEL) * sb_scale
    sb_w_out = nrm(ks[6], (N_SB_LAYERS, D_MODEL, D_MODEL), D_MODEL, DEEPNORM_BETA)
    ssd_cols = jnp.arange(SSD_IN_DIM)
    ssd_scale = jnp.where((ssd_cols >= SSD_D_INNER) & (ssd_cols < 2 * SSD_D_INNER),
                          DEEPNORM_BETA, 1.0).astype(f32)
    ssd_w_in = nrm(ks[7], (N_SSD_LAYERS, D_MODEL, SSD_IN_DIM), D_MODEL) * ssd_scale
    ssd_conv_w = nrm(ks[8], (N_SSD_LAYERS, SSD_CONV, SSD_CONV_DIM), SSD_CONV)
    ssd_conv_b = 0.02 * jax.random.normal(ks[9], (N_SSD_LAYERS, SSD_CONV_DIM), f32)
    dt0 = jnp.exp(jax.random.uniform(ks[10], (N_SSD_LAYERS, SSD_HEADS), f32,
                                     math.log(1e-3), math.log(1e-1)))
    ssd_dt_bias = dt0 + jnp.log(-jnp.expm1(-dt0))
    ssd_a_log = jnp.log(jax.random.uniform(ks[11], (N_SSD_LAYERS, SSD_HEADS), f32, 1.0, 16.0))
    ssd_d = 1.0 + 0.02 * jax.random.normal(ks[12], (N_SSD_LAYERS, SSD_HEADS), f32)
    ssd_norm_g = 1.0 + 0.02 * jax.random.normal(ks[13], (N_SSD_LAYERS, SSD_D_INNER), f32)
    ssd_w_out = nrm(ks[14], (N_SSD_LAYERS, SSD_D_INNER, D_MODEL), SSD_D_INNER, DEEPNORM_BETA)
    return {"x": x, "ffn_w_gate_up": ffn_w_gate_up, "ffn_w_down": ffn_w_down,
            "ln_g": ln_g, "ln_b": ln_b, "sb_w_in": sb_w_in, "sb_w_out": sb_w_out,
            "ssd_w_in": ssd_w_in, "ssd_conv_w": ssd_conv_w, "ssd_conv_b": ssd_conv_b,
            "ssd_dt_bias": ssd_dt_bias, "ssd_a_log": ssd_a_log, "ssd_d": ssd_d,
            "ssd_norm_g": ssd_norm_g, "ssd_w_out": ssd_w_out}


def reference(x, ffn_w_gate_up, ffn_w_down, ln_g, ln_b, sb_w_in, sb_w_out,
              ssd_w_in, ssd_conv_w, ssd_conv_b, ssd_dt_bias, ssd_a_log, ssd_d,
              ssd_norm_g, ssd_w_out):
    for i in range(DEPTH):
        mixer = i % N_MIXERS
        j = i // N_MIXERS
        x = layer_norm(DEEPNORM_ALPHA * x + 0.5 * swiglu(x, ffn_w_gate_up[i, 0], ffn_w_down[i, 0]),
                       ln_g[i, 0], ln_b[i, 0])
        if mixer == 0:
            mix = stick_breaking_attention(x, sb_w_in[j], sb_w_out[j])
        else:
            mix = ssd_mixer(x, ssd_w_in[j], ssd_conv_w[j], ssd_conv_b[j], ssd_dt_bias[j],
                            ssd_a_log[j], ssd_d[j], ssd_norm_g[j], ssd_w_out[j])
        x = layer_norm(DEEPNORM_ALPHA * x + mix, ln_g[i, 1], ln_b[i, 1])
        x = layer_norm(DEEPNORM_ALPHA * x + 0.5 * swiglu(x, ffn_w_gate_up[i, 1], ffn_w_down[i, 1]),
                       ln_g[i, 2], ln_b[i, 2])
    return x
```

```python
import functools

import jax
import jax.numpy as jnp
from jax import lax
from jax.experimental import pallas as pl
from jax.experimental.pallas import tpu as pltpu

F32 = jnp.float32
BF16 = jnp.bfloat16

LN_EPS = 1e-5
RMS_EPS = 1e-5
DEPTH = 2
DEEPNORM_ALPHA = (2.0 * DEPTH) ** 0.25

SB_HEADS = 16
SSD_HEAD_DIM = 64
SSD_GROUPS = 8
SSD_STATE = 128
SSD_CONV = 4
CONV_TAIL = 8

LANES = 128
VMEM_LIMIT = 56 * 1024 * 1024


def _params(sem):
    return pltpu.CompilerParams(dimension_semantics=sem, vmem_limit_bytes=VMEM_LIMIT)


def _layer_norm(r, g, b):
    mu = jnp.mean(r, axis=-1, keepdims=True)
    c = r - mu
    var = jnp.mean(c * c, axis=-1, keepdims=True)
    return c * lax.rsqrt(var + LN_EPS) * g + b


def _softplus(x):
    return jnp.maximum(x, 0.0) + jnp.log1p(jnp.exp(-jnp.abs(x)))


def _silu(x):
    return x * jax.nn.sigmoid(x)


def _split_bf16(x, parts):
    out = []
    for _ in range(parts - 1):
        p = x.astype(BF16)
        out.append(p)
        x = x - p.astype(F32)
    out.append(x.astype(BF16))
    return out


def _ffn_ln_kernel(xb_ref, xf_ref, wg_ref, wu_ref, wd_ref, g_ref, b_ref, yf_ref, yb_ref, acc_ref, *, nf):
    j = pl.program_id(1)
    xb = xb_ref[...]
    gate = jnp.dot(xb, wg_ref[...], preferred_element_type=F32)
    up = jnp.dot(xb, wu_ref[...], preferred_element_type=F32)
    h = (_silu(gate) * up).astype(BF16)
    part = jnp.dot(h, wd_ref[...], preferred_element_type=F32)

    @pl.when(j == 0)
    def _():
        acc_ref[...] = part

    @pl.when(j > 0)
    def _():
        acc_ref[...] += part

    @pl.when(j == nf - 1)
    def _():
        r = DEEPNORM_ALPHA * xf_ref[...] + 0.5 * acc_ref[...]
        y = _layer_norm(r, g_ref[...], b_ref[...])
        yf_ref[...] = y
        yb_ref[...] = y.astype(BF16)


def ffn_ln(xb, xf, w_gate_up, w_down, g, b, *, tm, tf):
    m, d = xf.shape
    f = w_down.shape[0]
    nf = f // tf
    assert m % tm == 0 and f % tf == 0 and w_gate_up.shape == (d, 2 * f)
    return pl.pallas_call(
        functools.partial(_ffn_ln_kernel, nf=nf),
        grid=(m // tm, nf),
        in_specs=[
            pl.BlockSpec((tm, d), lambda i, j: (i, 0)),
            pl.BlockSpec((tm, d), lambda i, j: (i, 0)),
            pl.BlockSpec((d, tf), lambda i, j: (0, j)),
            pl.BlockSpec((d, tf), lambda i, j: (0, j + nf)),
            pl.BlockSpec((tf, d), lambda i, j: (j, 0)),
            pl.BlockSpec((1, d), lambda i, j: (0, 0)),
            pl.BlockSpec((1, d), lambda i, j: (0, 0)),
        ],
        out_specs=[
            pl.BlockSpec((tm, d), lambda i, j: (i, 0)),
            pl.BlockSpec((tm, d), lambda i, j: (i, 0)),
        ],
        out_shape=[jax.ShapeDtypeStruct((m, d), F32), jax.ShapeDtypeStruct((m, d), BF16)],
        scratch_shapes=[pltpu.VMEM((tm, d), F32)],
        compiler_params=_params(("parallel", "arbitrary")),
        name="ffn_ln",
    )(xb, xf, w_gate_up, w_gate_up, w_down, g, b)


def _matmul_kernel(x_ref, w_ref, o_ref):
    o_ref[...] = jnp.dot(x_ref[...], w_ref[...], preferred_element_type=F32).astype(o_ref.dtype)


def matmul(x, w, out_dtype, *, tm, tn):
    m, k = x.shape
    n = w.shape[1]
    assert m % tm == 0 and n % tn == 0
    return pl.pallas_call(
        _matmul_kernel,
        grid=(n // tn, m // tm),
        in_specs=[
            pl.BlockSpec((tm, k), lambda j, i: (i, 0)),
            pl.BlockSpec((k, tn), lambda j, i: (0, j)),
        ],
        out_specs=pl.BlockSpec((tm, tn), lambda j, i: (i, j)),
        out_shape=jax.ShapeDtypeStruct((m, n), out_dtype),
        compiler_params=_params(("parallel", "parallel")),
        name="in_proj",
    )(x, w)


def _proj_ln_kernel(a_ref, w_ref, xf_ref, g_ref, b_ref, yf_ref, yb_ref, acc_ref, *, nk):
    k = pl.program_id(1)
    part = jnp.dot(a_ref[...], w_ref[...], preferred_element_type=F32)

    @pl.when(k == 0)
    def _():
        acc_ref[...] = part

    @pl.when(k > 0)
    def _():
        acc_ref[...] += part

    @pl.when(k == nk - 1)
    def _():
        r = DEEPNORM_ALPHA * xf_ref[...] + acc_ref[...]
        y = _layer_norm(r, g_ref[...], b_ref[...])
        yf_ref[...] = y
        yb_ref[...] = y.astype(BF16)


def proj_ln(a, w, xf, g, b, *, tm, tk):
    m, kdim = a.shape
    d = w.shape[1]
    nk = kdim // tk
    assert m % tm == 0 and kdim % tk == 0
    return pl.pallas_call(
        functools.partial(_proj_ln_kernel, nk=nk),
        grid=(m // tm, nk),
        in_specs=[
            pl.BlockSpec((tm, tk), lambda i, k: (i, k)),
            pl.BlockSpec((tk, d), lambda i, k: (k, 0)),
            pl.BlockSpec((tm, d), lambda i, k: (i, 0)),
            pl.BlockSpec((1, d), lambda i, k: (0, 0)),
            pl.BlockSpec((1, d), lambda i, k: (0, 0)),
        ],
        out_specs=[
            pl.BlockSpec((tm, d), lambda i, k: (i, 0)),
            pl.BlockSpec((tm, d), lambda i, k: (i, 0)),
        ],
        out_shape=[jax.ShapeDtypeStruct((m, d), F32), jax.ShapeDtypeStruct((m, d), BF16)],
        scratch_shapes=[pltpu.VMEM((tm, d), F32)],
        compiler_params=_params(("parallel", "arbitrary")),
        name="proj_ln",
    )(a, w, xf, g, b)


def _sb_attn_kernel(q_ref, k_ref, v_ref, u_ref, o_ref, acc_ref, carry_ref, *, tq, tk, scale):
    qi = pl.program_id(2)
    q = q_ref[...]
    u2 = u_ref[...]
    acc_ref[...] = jnp.zeros_like(acc_ref)
    carry_ref[...] = jnp.zeros_like(carry_ref)
    n_diag = tq // tk

    def sweep(kb, masked):
        start = pl.multiple_of(kb * tk, tk)
        kblk = k_ref[pl.ds(start, tk), :]
        vblk = v_ref[pl.ds(start, tk), :]
        z = lax.dot_general(q, kblk, (((1,), (1,)), ((), ())), preferred_element_type=F32) * scale
        log_keep = -_softplus(z)
        if masked:
            q_pos = qi * tq + lax.broadcasted_iota(jnp.int32, (tq, tk), 0)
            k_pos = kb * tk + lax.broadcasted_iota(jnp.int32, (tq, tk), 1)
            strict = k_pos < q_pos
            log_keep = jnp.where(strict, log_keep, 0.0)
        hi, lo = _split_bf16(log_keep, 2)
        r = jnp.dot(jnp.concatenate([hi, lo], axis=1), u2, preferred_element_type=F32)
        carry = carry_ref[...]
        att = jnp.exp(z + log_keep + r[:, :tk] + carry)
        if masked:
            att = jnp.where(strict, att, 0.0)
        acc_ref[...] += jnp.dot(att.astype(BF16), vblk, preferred_element_type=F32)
        carry_ref[...] = carry + r[:, tk:]

    for d in range(n_diag):
        sweep(qi * n_diag + (n_diag - 1 - d), True)

    n_full = qi * n_diag

    def body(i, c):
        sweep(n_full - 1 - i, False)
        return c

    lax.fori_loop(0, n_full, body, 0)
    o_ref[...] = acc_ref[...].astype(o_ref.dtype)


def sb_attention(qkv, *, batch, seq, heads, tq, tk):
    dh = qkv.shape[1] // (3 * heads)
    assert dh == tk == LANES and seq % tq == 0 and tq % tk == 0
    qkv3 = qkv.reshape(batch, seq, 3 * heads * dh)
    row = lax.broadcasted_iota(jnp.int32, (tk, 2 * tk), 0)
    col = lax.broadcasted_iota(jnp.int32, (tk, 2 * tk), 1)
    u = ((row > col) | (col >= tk)).astype(BF16)
    u2 = jnp.concatenate([u, u], axis=0)
    out = pl.pallas_call(
        functools.partial(_sb_attn_kernel, tq=tq, tk=tk, scale=dh ** -0.5),
        grid=(batch, heads, seq // tq),
        in_specs=[
            pl.BlockSpec((None, tq, dh), lambda b, h, i: (b, i, h)),
            pl.BlockSpec((None, seq, dh), lambda b, h, i: (b, 0, heads + h)),
            pl.BlockSpec((None, seq, dh), lambda b, h, i: (b, 0, 2 * heads + h)),
            pl.BlockSpec((2 * tk, 2 * tk), lambda b, h, i: (0, 0)),
        ],
        out_specs=pl.BlockSpec((None, tq, dh), lambda b, h, i: (b, i, h)),
        out_shape=jax.ShapeDtypeStruct((batch, seq, heads * dh), BF16),
        scratch_shapes=[pltpu.VMEM((tq, dh), F32), pltpu.VMEM((tq, tk), F32)],
        compiler_params=_params(("parallel", "parallel", "arbitrary")),
        name="sb_attn",
    )(qkv3, qkv3, qkv3, u2)
    return out.reshape(batch * seq, heads * dh)


def _ssd_kernel(z_ref, x_ref, bc_ref, dt_ref, cwx_ref, cbx_ref, cwbc_ref, cbbc_ref, dtb_ref, alog_ref,
                dskip_ref, ng_ref, y_ref, xbuf_ref, bcbuf_ref, state_ref, yacc_ref, *, chunk, groups, hpg, p, n):
    c = pl.program_id(1)
    L = chunk
    T = CONV_TAIL

    @pl.when(c == 0)
    def _():
        xbuf_ref[0:T, :] = jnp.zeros((T, xbuf_ref.shape[1]), F32)
        bcbuf_ref[0:T, :] = jnp.zeros((T, bcbuf_ref.shape[1]), F32)
        state_ref[...] = jnp.zeros_like(state_ref)

    def conv_silu(buf_ref, cur_ref, w_ref, b_ref):
        buf_ref[T:T + L, :] = cur_ref[...]
        acc = b_ref[...]
        for k in range(SSD_CONV):
            off = T - (SSD_CONV - 1) + k
            acc = acc + w_ref[k:k + 1, :] * buf_ref[off:off + L, :]
        buf_ref[0:T, :] = buf_ref[L:L + T, :]
        return _silu(acc)

    xs = conv_silu(xbuf_ref, x_ref, cwx_ref, cbx_ref)
    bc = conv_silu(bcbuf_ref, bc_ref, cwbc_ref, cbbc_ref)

    dt = _softplus(dt_ref[...] + dtb_ref[...])
    a = -jnp.exp(alog_ref[...])
    row = lax.broadcasted_iota(jnp.int32, (L, L), 0)
    col = lax.broadcasted_iota(jnp.int32, (L, L), 1)
    causal = col <= row
    tri = causal.astype(BF16)
    a_cum = sum(jnp.dot(tri, part, preferred_element_type=F32) for part in _split_bf16(dt * a, 3))
    a_cum_t = a_cum.T
    dt_t = dt.T
    e_cum = jnp.exp(a_cum)
    w_end = jnp.exp(a_cum[L - 1:L, :] - a_cum) * dt
    lane_lo = lax.broadcasted_iota(jnp.int32, (L, 2 * p), 1) < p
    lane_lo_n = lax.broadcasted_iota(jnp.int32, (n, 2 * p), 1) < p

    for g in range(groups):
        bm = bc[:, g * n:(g + 1) * n]
        cm = bc[:, (groups + g) * n:(groups + g + 1) * n].astype(BF16)
        cb = lax.dot_general(cm, bm.astype(BF16), (((1,), (1,)), ((), ())), preferred_element_type=F32)
        bm_t = bm.T.astype(BF16)
        for pair in range(hpg // 2):
            h0 = g * hpg + 2 * pair
            lanes = slice(h0 * p, (h0 + 2) * p)
            x2 = xs[:, lanes]
            x2b = x2.astype(BF16)
            ys = []
            for h in (h0, h0 + 1):
                seg = a_cum[:, h:h + 1] - a_cum_t[h:h + 1, :]
                m = cb * jnp.exp(jnp.where(causal, seg, -jnp.inf)) * dt_t[h:h + 1, :]
                ys.append(jnp.dot(m.astype(BF16), x2b, preferred_element_type=F32))
            y_diag = jnp.where(lane_lo, ys[0], ys[1])
            prev = state_ref[:, lanes]
            e2 = jnp.where(lane_lo, e_cum[:, h0:h0 + 1], e_cum[:, h0 + 1:h0 + 2])
            y_off = jnp.dot(cm, prev.astype(BF16), preferred_element_type=F32) * e2
            w2 = jnp.where(lane_lo, w_end[:, h0:h0 + 1], w_end[:, h0 + 1:h0 + 2])
            st = jnp.dot(bm_t, (x2 * w2).astype(BF16), preferred_element_type=F32)
            dec = jnp.where(lane_lo_n, e_cum[L - 1:L, h0:h0 + 1], e_cum[L - 1:L, h0 + 1:h0 + 2])
            state_ref[:, lanes] = dec * prev + st
            yacc_ref[:, lanes] = y_diag + y_off + x2 * dskip_ref[:, lanes]

    gw = hpg * p
    for g in range(groups):
        lanes = slice(g * gw, (g + 1) * gw)
        yg = yacc_ref[:, lanes] * _silu(z_ref[:, lanes])
        ms = jnp.mean(yg * yg, axis=-1, keepdims=True)
        y_ref[:, lanes] = (yg * lax.rsqrt(ms + RMS_EPS) * ng_ref[:, lanes]).astype(y_ref.dtype)


def ssd_scan(proj, conv_w, conv_b, dt_bias, a_log, d_skip, norm_g, *, batch, seq, chunk, d_inner):
    groups, p, n = SSD_GROUPS, SSD_HEAD_DIM, SSD_STATE
    heads = d_inner // p
    hpg = heads // groups
    bcw = 2 * groups * n
    assert chunk == LANES and n == LANES and 2 * p == LANES and heads <= LANES and hpg % 2 == 0
    assert seq % chunk == 0 and d_inner % bcw == 0
    nc = seq // chunk
    pad = LANES - heads
    row2 = lambda v: v.reshape(1, -1)
    cwx, cwbc = conv_w[:, :d_inner], conv_w[:, d_inner:]
    cbx, cbbc = row2(conv_b[:d_inner]), row2(conv_b[d_inner:])
    dtb = row2(jnp.pad(dt_bias, (0, pad)))
    alog = row2(jnp.pad(a_log, (0, pad)))
    dskip = row2(jnp.repeat(d_skip, p))
    ng = row2(norm_g)
    x_blk = d_inner // d_inner
    bc_blk = 2 * d_inner // bcw
    dt_blk = (2 * d_inner + bcw) // LANES
    tok = lambda b, c: b * nc + c
    const = lambda shape: pl.BlockSpec(shape, lambda b, c: (0, 0))
    return pl.pallas_call(
        functools.partial(_ssd_kernel, chunk=chunk, groups=groups, hpg=hpg, p=p, n=n),
        grid=(batch, nc),
        in_specs=[
            pl.BlockSpec((chunk, d_inner), lambda b, c: (tok(b, c), 0)),
            pl.BlockSpec((chunk, d_inner), lambda b, c: (tok(b, c), x_blk)),
            pl.BlockSpec((chunk, bcw), lambda b, c: (tok(b, c), bc_blk)),
            pl.BlockSpec((chunk, LANES), lambda b, c: (tok(b, c), dt_blk)),
            const((SSD_CONV, d_inner)), const((1, d_inner)),
            const((SSD_CONV, bcw)), const((1, bcw)),
            const((1, LANES)), const((1, LANES)),
            const((1, d_inner)), const((1, d_inner)),
        ],
        out_specs=pl.BlockSpec((chunk, d_inner), lambda b, c: (tok(b, c), 0)),
        out_shape=jax.ShapeDtypeStruct((batch * seq, d_inner), BF16),
        scratch_shapes=[
            pltpu.VMEM((chunk + CONV_TAIL, d_inner), F32),
            pltpu.VMEM((chunk + CONV_TAIL, bcw), F32),
            pltpu.VMEM((n, d_inner), F32),
            pltpu.VMEM((chunk, d_inner), F32),
        ],
        compiler_params=_params(("parallel", "arbitrary")),
        name="ssd_scan",
    )(proj, proj, proj, proj, cwx, cbx, cwbc, cbbc, dtb, alog, dskip, ng)


FFN_TM, FFN_TF = 512, 512
PROJ_TM, PROJ_TN = 1024, 1024
OUT_TM, OUT_TK = 512, 2048
SB_TQ, SB_TK = 256, 128
SSD_CHUNK = 128
SSD_PROJ_TN = 1152


def kernel(x, ffn_w_gate_up, ffn_w_down, ln_g, ln_b, sb_w_in, sb_w_out, ssd_w_in, ssd_conv_w, ssd_conv_b,
           ssd_dt_bias, ssd_a_log, ssd_d, ssd_norm_g, ssd_w_out):
    batch, seq, d = x.shape
    m = batch * seq
    depth = ffn_w_gate_up.shape[0]
    assert depth == DEPTH
    xf = x.reshape(m, d)
    xb = xf.astype(BF16)
    row2 = lambda v: v.reshape(1, -1)

    def ffn(xb, xf, i, s):
        return ffn_ln(xb, xf, ffn_w_gate_up[i, s].astype(BF16), ffn_w_down[i, s].astype(BF16),
                      row2(ln_g[i, 2 * s]), row2(ln_b[i, 2 * s]), tm=FFN_TM, tf=FFN_TF)

    for i in range(depth):
        j = i // 2
        xf, xb = ffn(xb, xf, i, 0)
        if i % 2 == 0:
            qkv = matmul(xb, sb_w_in[j].astype(BF16), BF16, tm=PROJ_TM, tn=PROJ_TN)
            mix = sb_attention(qkv, batch=batch, seq=seq, heads=SB_HEADS, tq=SB_TQ, tk=SB_TK)
            w_out = sb_w_out[j]
        else:
            d_inner = ssd_w_out.shape[1]
            w_in = ssd_w_in[j]
            n_pad = -w_in.shape[1] % SSD_PROJ_TN
            w_in = jnp.pad(w_in, ((0, 0), (0, n_pad))).astype(BF16)
            proj = matmul(xb, w_in, F32, tm=PROJ_TM, tn=SSD_PROJ_TN)
            mix = ssd_scan(proj, ssd_conv_w[j], ssd_conv_b[j], ssd_dt_bias[j], ssd_a_log[j], ssd_d[j],
                           ssd_norm_g[j], batch=batch, seq=seq, chunk=SSD_CHUNK, d_inner=d_inner)
            w_out = ssd_w_out[j]
        xf, xb = proj_ln(mix, w_out.astype(BF16), xf, row2(ln_g[i, 1]), row2(ln_b[i, 1]),
                         tm=OUT_TM, tk=OUT_TK)
        xf, xb = ffn(xb, xf, i, 1)
    return xf.reshape(batch, seq, d)
```

```python
import functools

import jax
import jax.numpy as jnp
from jax import lax
from jax.experimental import pallas as pl
from jax.experimental.pallas import tpu as pltpu

F32 = jnp.float32
BF16 = jnp.bfloat16

LOG2_E = 1.4426950408889634
LN_EPS = 1e-5
RMS_EPS = 1e-5
DEPTH = 2
DEEPNORM_ALPHA = (2.0 * DEPTH) ** 0.25

SB_HEADS = 16
SSD_HEAD_DIM = 64
SSD_GROUPS = 8
SSD_STATE = 128
SSD_CONV = 4
CONV_TAIL = 8

LANES = 128
VMEM_LIMIT = 56 * 1024 * 1024


def _params(sem):
    return pltpu.CompilerParams(dimension_semantics=sem, vmem_limit_bytes=VMEM_LIMIT)


def _layer_norm(r, g, b):
    mu = jnp.mean(r, axis=-1, keepdims=True)
    c = r - mu
    var = jnp.mean(c * c, axis=-1, keepdims=True)
    return c * lax.rsqrt(var + LN_EPS) * g + b


def _softplus(x):
    return jnp.maximum(x, 0.0) + jnp.log1p(jnp.exp(-jnp.abs(x)))


def _silu(x):
    return x * jax.nn.sigmoid(x)


def _split_bf16(x, parts):
    out = []
    for _ in range(parts - 1):
        p = x.astype(BF16)
        out.append(p)
        x = x - p.astype(F32)
    out.append(x.astype(BF16))
    return out


def _ffn_ln_kernel(xb_ref, xf_ref, wg_ref, wu_ref, wd_ref, g_ref, b_ref, yf_ref, yb_ref, acc_ref, *, nf):
    j = pl.program_id(1)

    @pl.when(j == 0)
    def _():
        acc_ref[...] = jnp.zeros_like(acc_ref)

    xb = xb_ref[...]
    gate = jnp.dot(xb, wg_ref[...], preferred_element_type=F32)
    up = jnp.dot(xb, wu_ref[...], preferred_element_type=F32)
    h = (_silu(gate) * up).astype(BF16)
    acc_ref[...] += jnp.dot(h, wd_ref[...], preferred_element_type=F32)

    @pl.when(j == nf - 1)
    def _():
        r = DEEPNORM_ALPHA * xf_ref[...] + 0.5 * acc_ref[...]
        y = _layer_norm(r, g_ref[...], b_ref[...])
        yf_ref[...] = y
        yb_ref[...] = y.astype(BF16)


def ffn_ln(xb, xf, w_gate_up, w_down, g, b, *, tm, tf):
    m, d = xf.shape
    f = w_down.shape[0]
    nf = f // tf
    assert m % tm == 0 and f % tf == 0 and w_gate_up.shape == (d, 2 * f)
    return pl.pallas_call(
        functools.partial(_ffn_ln_kernel, nf=nf),
        grid=(m // tm, nf),
        in_specs=[
            pl.BlockSpec((tm, d), lambda i, j: (i, 0)),
            pl.BlockSpec((tm, d), lambda i, j: (i, 0)),
            pl.BlockSpec((d, tf), lambda i, j: (0, j)),
            pl.BlockSpec((d, tf), lambda i, j: (0, j + nf)),
            pl.BlockSpec((tf, d), lambda i, j: (j, 0)),
            pl.BlockSpec((1, d), lambda i, j: (0, 0)),
            pl.BlockSpec((1, d), lambda i, j: (0, 0)),
        ],
        out_specs=[
            pl.BlockSpec((tm, d), lambda i, j: (i, 0)),
            pl.BlockSpec((tm, d), lambda i, j: (i, 0)),
        ],
        out_shape=[jax.ShapeDtypeStruct((m, d), F32), jax.ShapeDtypeStruct((m, d), BF16)],
        scratch_shapes=[pltpu.VMEM((tm, d), F32)],
        compiler_params=_params(("parallel", "arbitrary")),
        name="ffn_ln",
    )(xb, xf, w_gate_up, w_gate_up, w_down, g, b)


def _matmul_kernel(x_ref, w_ref, o_ref):
    o_ref[...] = jnp.dot(x_ref[...], w_ref[...], preferred_element_type=F32).astype(o_ref.dtype)


def matmul(x, w, out_dtype, *, tm, tn):
    m, k = x.shape
    n = w.shape[1]
    assert m % tm == 0 and n % tn == 0
    return pl.pallas_call(
        _matmul_kernel,
        grid=(n // tn, m // tm),
        in_specs=[
            pl.BlockSpec((tm, k), lambda j, i: (i, 0)),
            pl.BlockSpec((k, tn), lambda j, i: (0, j)),
        ],
        out_specs=pl.BlockSpec((tm, tn), lambda j, i: (i, j)),
        out_shape=jax.ShapeDtypeStruct((m, n), out_dtype),
        compiler_params=_params(("parallel", "parallel")),
        name="in_proj",
    )(x, w)


def _proj_ln_kernel(a_ref, w_ref, xf_ref, g_ref, b_ref, yf_ref, yb_ref, acc_ref, *, nk):
    k = pl.program_id(1)

    @pl.when(k == 0)
    def _():
        acc_ref[...] = jnp.zeros_like(acc_ref)

    acc_ref[...] += jnp.dot(a_ref[...], w_ref[...], preferred_element_type=F32)

    @pl.when(k == nk - 1)
    def _():
        r = DEEPNORM_ALPHA * xf_ref[...] + acc_ref[...]
        y = _layer_norm(r, g_ref[...], b_ref[...])
        yf_ref[...] = y
        yb_ref[...] = y.astype(BF16)


def proj_ln(a, w, xf, g, b, *, tm, tk):
    m, kdim = a.shape
    d = w.shape[1]
    nk = kdim // tk
    assert m % tm == 0 and kdim % tk == 0
    return pl.pallas_call(
        functools.partial(_proj_ln_kernel, nk=nk),
        grid=(m // tm, nk),
        in_specs=[
            pl.BlockSpec((tm, tk), lambda i, k: (i, k)),
            pl.BlockSpec((tk, d), lambda i, k: (k, 0)),
            pl.BlockSpec((tm, d), lambda i, k: (i, 0)),
            pl.BlockSpec((1, d), lambda i, k: (0, 0)),
            pl.BlockSpec((1, d), lambda i, k: (0, 0)),
        ],
        out_specs=[
            pl.BlockSpec((tm, d), lambda i, k: (i, 0)),
            pl.BlockSpec((tm, d), lambda i, k: (i, 0)),
        ],
        out_shape=[jax.ShapeDtypeStruct((m, d), F32), jax.ShapeDtypeStruct((m, d), BF16)],
        scratch_shapes=[pltpu.VMEM((tm, d), F32)],
        compiler_params=_params(("parallel", "arbitrary")),
        name="proj_ln",
    )(a, w, xf, g, b)


def _sb_attn_kernel(q_ref, k_ref, v_ref, u_ref, o_ref, acc_ref, carry_ref, z_ref, *, tq, tk, dh, hb, scale):
    qi = pl.program_id(2)
    u2 = u_ref[...]
    acc_ref[...] = jnp.zeros_like(acc_ref)
    carry_ref[...] = jnp.zeros_like(carry_ref)
    n_diag = tq // tk
    cols = [slice(h * dh, (h + 1) * dh) for h in range(hb)]
    to_log2 = scale * LOG2_E

    def logits(kb, rows):
        start = pl.multiple_of(kb * tk, tk)
        return [lax.dot_general(q_ref[rows, c], k_ref[pl.ds(start, tk), c], (((1,), (1,)), ((), ())),
                                preferred_element_type=F32) * to_log2 for c in cols]

    def cum_log_keep(zs, strict):
        rs = []
        for z in zs:
            sp = jnp.maximum(z, 0.0) + jnp.log2(1.0 + jnp.exp2(-jnp.abs(z)))
            if strict is not None:
                sp = jnp.where(strict, sp, 0.0)
            hi, lo = _split_bf16(sp, 2)
            rs.append(jnp.dot(jnp.concatenate([hi, lo], axis=1), u2, preferred_element_type=F32))
        return rs

    def accumulate(kb, rows, zs, rs, strict):
        start = pl.multiple_of(kb * tk, tk)
        for c, z, r in zip(cols, zs, rs):
            carry = carry_ref[rows, c]
            att = jnp.exp2(z + r[:, :tk] + carry)
            if strict is not None:
                att = jnp.where(strict, att, 0.0)
            acc_ref[rows, c] += jnp.dot(att.astype(BF16), v_ref[pl.ds(start, tk), c],
                                        preferred_element_type=F32)
            carry_ref[rows, c] = carry + r[:, tk:]

    for j in reversed(range(n_diag)):
        kb = qi * n_diag + j
        rows = slice(j * tk, tq)
        q_pos = qi * tq + j * tk + lax.broadcasted_iota(jnp.int32, (tq - j * tk, tk), 0)
        k_pos = kb * tk + lax.broadcasted_iota(jnp.int32, (tq - j * tk, tk), 1)
        strict = k_pos < q_pos
        zs = logits(kb, rows)
        accumulate(kb, rows, zs, cum_log_keep(zs, strict), strict)

    n_full = qi * n_diag
    full = slice(0, tq)
    for c, z in zip(cols, logits(jnp.maximum(n_full - 1, 0), full)):
        z_ref[:, c] = z

    def body(i, carry_unused):
        kb = n_full - 1 - i
        zs = [z_ref[:, c] for c in cols]
        rs = cum_log_keep(zs, None)
        z_next = logits(jnp.maximum(kb - 1, 0), full)
        accumulate(kb, full, zs, rs, None)
        for c, z in zip(cols, z_next):
            z_ref[:, c] = z
        return carry_unused

    lax.fori_loop(0, n_full, body, 0)
    o_ref[...] = acc_ref[...].astype(o_ref.dtype)


def sb_attention(qkv, *, batch, seq, heads, tq, tk, hb):
    dh = qkv.shape[1] // (3 * heads)
    assert dh == tk == LANES and seq % tq == 0 and tq % tk == 0 and heads % hb == 0
    qkv3 = qkv.reshape(batch, seq, 3 * heads * dh)
    row = lax.broadcasted_iota(jnp.int32, (tk, 2 * tk), 0)
    col = lax.broadcasted_iota(jnp.int32, (tk, 2 * tk), 1)
    u = -((row >= col) | (col >= tk)).astype(BF16)
    u2 = jnp.concatenate([u, u], axis=0)
    nhb = heads // hb
    out = pl.pallas_call(
        functools.partial(_sb_attn_kernel, tq=tq, tk=tk, dh=dh, hb=hb, scale=dh ** -0.5),
        grid=(batch, nhb, seq // tq),
        in_specs=[
            pl.BlockSpec((None, tq, hb * dh), lambda b, h, i: (b, i, h)),
            pl.BlockSpec((None, seq, hb * dh), lambda b, h, i: (b, 0, nhb + h)),
            pl.BlockSpec((None, seq, hb * dh), lambda b, h, i: (b, 0, 2 * nhb + h)),
            pl.BlockSpec((2 * tk, 2 * tk), lambda b, h, i: (0, 0)),
        ],
        out_specs=pl.BlockSpec((None, tq, hb * dh), lambda b, h, i: (b, i, h)),
        out_shape=jax.ShapeDtypeStruct((batch, seq, heads * dh), BF16),
        scratch_shapes=[pltpu.VMEM((tq, hb * dh), F32), pltpu.VMEM((tq, hb * tk), F32),
                        pltpu.VMEM((tq, hb * tk), F32)],
        compiler_params=_params(("parallel", "parallel", "arbitrary")),
        name="sb_attn",
    )(qkv3, qkv3, qkv3, u2)
    return out.reshape(batch * seq, heads * dh)


def _ssd_kernel(z_ref, x_ref, bc_ref, dt_ref, cwx_ref, cbx_ref, cwbc_ref, cbbc_ref, dtb_ref, alog_ref,
                dskip_ref, ng_ref, y_ref, xbuf_ref, bcbuf_ref, state_ref, yacc_ref, *, chunk, groups, hpg, p, n):
    c = pl.program_id(1)
    L = chunk
    T = CONV_TAIL

    @pl.when(c == 0)
    def _():
        xbuf_ref[0:T, :] = jnp.zeros((T, xbuf_ref.shape[1]), F32)
        bcbuf_ref[0:T, :] = jnp.zeros((T, bcbuf_ref.shape[1]), F32)
        state_ref[...] = jnp.zeros_like(state_ref)

    def conv_silu(buf_ref, cur_ref, w_ref, b_ref):
        buf_ref[T:T + L, :] = cur_ref[...]
        acc = b_ref[...]
        for k in range(SSD_CONV):
            off = T - (SSD_CONV - 1) + k
            acc = acc + w_ref[k:k + 1, :] * buf_ref[off:off + L, :]
        buf_ref[0:T, :] = buf_ref[L:L + T, :]
        return _silu(acc)

    xs = conv_silu(xbuf_ref, x_ref, cwx_ref, cbx_ref)
    bc = conv_silu(bcbuf_ref, bc_ref, cwbc_ref, cbbc_ref)

    dt = _softplus(dt_ref[...] + dtb_ref[...])
    a = -jnp.exp(alog_ref[...])
    row = lax.broadcasted_iota(jnp.int32, (L, L), 0)
    col = lax.broadcasted_iota(jnp.int32, (L, L), 1)
    causal = col <= row
    tri = causal.astype(BF16)
    a_cum = sum(jnp.dot(tri, part, preferred_element_type=F32) for part in _split_bf16(dt * a, 3))
    a_cum_t = a_cum.T
    dt_t = dt.T
    e_cum = jnp.exp(a_cum)
    w_end = jnp.exp(a_cum[L - 1:L, :] - a_cum) * dt
    lane_lo = lax.broadcasted_iota(jnp.int32, (L, 2 * p), 1) < p
    lane_lo_n = lax.broadcasted_iota(jnp.int32, (n, 2 * p), 1) < p

    for g in range(groups):
        bm = bc[:, g * n:(g + 1) * n]
        cm = bc[:, (groups + g) * n:(groups + g + 1) * n].astype(BF16)
        cb = lax.dot_general(cm, bm.astype(BF16), (((1,), (1,)), ((), ())), preferred_element_type=F32)
        bm_t = bm.T.astype(BF16)
        for pair in range(hpg // 2):
            h0 = g * hpg + 2 * pair
            lanes = slice(h0 * p, (h0 + 2) * p)
            x2 = xs[:, lanes]
            x2b = x2.astype(BF16)
            ys = []
            for h in (h0, h0 + 1):
                seg = a_cum[:, h:h + 1] - a_cum_t[h:h + 1, :]
                m = cb * jnp.exp(jnp.where(causal, seg, -jnp.inf)) * dt_t[h:h + 1, :]
                ys.append(jnp.dot(m.astype(BF16), x2b, preferred_element_type=F32))
            y_diag = jnp.where(lane_lo, ys[0], ys[1])
            prev = state_ref[:, lanes]
            e2 = jnp.where(lane_lo, e_cum[:, h0:h0 + 1], e_cum[:, h0 + 1:h0 + 2])
            y_off = jnp.dot(cm, prev.astype(BF16), preferred_element_type=F32) * e2
            w2 = jnp.where(lane_lo, w_end[:, h0:h0 + 1], w_end[:, h0 + 1:h0 + 2])
            st = jnp.dot(bm_t, (x2 * w2).astype(BF16), preferred_element_type=F32)
            dec = jnp.where(lane_lo_n, e_cum[L - 1:L, h0:h0 + 1], e_cum[L - 1:L, h0 + 1:h0 + 2])
            state_ref[:, lanes] = dec * prev + st
            yacc_ref[:, lanes] = y_diag + y_off + x2 * dskip_ref[:, lanes]

    gw = hpg * p
    for g in range(groups):
        lanes = slice(g * gw, (g + 1) * gw)
        yg = yacc_ref[:, lanes] * _silu(z_ref[:, lanes])
        ms = jnp.mean(yg * yg, axis=-1, keepdims=True)
        y_ref[:, lanes] = (yg * lax.rsqrt(ms + RMS_EPS) * ng_ref[:, lanes]).astype(y_ref.dtype)


def ssd_scan(proj, conv_w, conv_b, dt_bias, a_log, d_skip, norm_g, *, batch, seq, chunk, d_inner):
    groups, p, n = SSD_GROUPS, SSD_HEAD_DIM, SSD_STATE
    heads = d_inner // p
    hpg = heads // groups
    bcw = 2 * groups * n
    assert chunk == LANES and n == LANES and 2 * p == LANES and heads <= LANES and hpg % 2 == 0
    assert seq % chunk == 0 and d_inner % bcw == 0
    nc = seq // chunk
    pad = LANES - heads
    row2 = lambda v: v.reshape(1, -1)
    cwx, cwbc = conv_w[:, :d_inner], conv_w[:, d_inner:]
    cbx, cbbc = row2(conv_b[:d_inner]), row2(conv_b[d_inner:])
    dtb = row2(jnp.pad(dt_bias, (0, pad)))
    alog = row2(jnp.pad(a_log, (0, pad)))
    dskip = row2(jnp.repeat(d_skip, p))
    ng = row2(norm_g)
    x_blk = d_inner // d_inner
    bc_blk = 2 * d_inner // bcw
    dt_blk = (2 * d_inner + bcw) // LANES
    tok = lambda b, c: b * nc + c
    const = lambda shape: pl.BlockSpec(shape, lambda b, c: (0, 0))
    return pl.pallas_call(
        functools.partial(_ssd_kernel, chunk=chunk, groups=groups, hpg=hpg, p=p, n=n),
        grid=(batch, nc),
        in_specs=[
            pl.BlockSpec((chunk, d_inner), lambda b, c: (tok(b, c), 0)),
            pl.BlockSpec((chunk, d_inner), lambda b, c: (tok(b, c), x_blk)),
            pl.BlockSpec((chunk, bcw), lambda b, c: (tok(b, c), bc_blk)),
            pl.BlockSpec((chunk, LANES), lambda b, c: (tok(b, c), dt_blk)),
            const((SSD_CONV, d_inner)), const((1, d_inner)),
            const((SSD_CONV, bcw)), const((1, bcw)),
            const((1, LANES)), const((1, LANES)),
            const((1, d_inner)), const((1, d_inner)),
        ],
        out_specs=pl.BlockSpec((chunk, d_inner), lambda b, c: (tok(b, c), 0)),
        out_shape=jax.ShapeDtypeStruct((batch * seq, d_inner), BF16),
        scratch_shapes=[
            pltpu.VMEM((chunk + CONV_TAIL, d_inner), F32),
            pltpu.VMEM((chunk + CONV_TAIL, bcw), F32),
            pltpu.VMEM((n, d_inner), F32),
            pltpu.VMEM((chunk, d_inner), F32),
        ],
        compiler_params=_params(("parallel", "arbitrary")),
        name="ssd_scan",
    )(proj, proj, proj, proj, cwx, cbx, cwbc, cbbc, dtb, alog, dskip, ng)


FFN_TM, FFN_TF = 512, 512
PROJ_TM, PROJ_TN = 1024, 1024
OUT_TM, OUT_TK = 512, 2048
SB_TQ, SB_TK, SB_HB = 256, 128, 8
SSD_CHUNK = 128
SSD_PROJ_TN = 1152


def kernel(x, ffn_w_gate_up, ffn_w_down, ln_g, ln_b, sb_w_in, sb_w_out, ssd_w_in, ssd_conv_w, ssd_conv_b,
           ssd_dt_bias, ssd_a_log, ssd_d, ssd_norm_g, ssd_w_out):
    batch, seq, d = x.shape
    m = batch * seq
    depth = ffn_w_gate_up.shape[0]
    assert depth == DEPTH
    xf = x.reshape(m, d)
    xb = xf.astype(BF16)
    row2 = lambda v: v.reshape(1, -1)

    def ffn(xb, xf, i, s):
        return ffn_ln(xb, xf, ffn_w_gate_up[i, s].astype(BF16), ffn_w_down[i, s].astype(BF16),
                      row2(ln_g[i, 2 * s]), row2(ln_b[i, 2 * s]), tm=FFN_TM, tf=FFN_TF)

    for i in range(depth):
        j = i // 2
        xf, xb = ffn(xb, xf, i, 0)
        if i % 2 == 0:
            qkv = matmul(xb, sb_w_in[j].astype(BF16), BF16, tm=PROJ_TM, tn=PROJ_TN)
            mix = sb_attention(qkv, batch=batch, seq=seq, heads=SB_HEADS, tq=SB_TQ, tk=SB_TK, hb=SB_HB)
            w_out = sb_w_out[j]
        else:
            d_inner = ssd_w_out.shape[1]
            w_in = ssd_w_in[j]
            n_pad = -w_in.shape[1] % SSD_PROJ_TN
            w_in = jnp.pad(w_in, ((0, 0), (0, n_pad))).astype(BF16)
            proj = matmul(xb, w_in, F32, tm=PROJ_TM, tn=SSD_PROJ_TN)
            mix = ssd_scan(proj, ssd_conv_w[j], ssd_conv_b[j], ssd_dt_bias[j], ssd_a_log[j], ssd_d[j],
                           ssd_norm_g[j], batch=batch, seq=seq, chunk=SSD_CHUNK, d_inner=d_inner)
            w_out = ssd_w_out[j]
        xf, xb = proj_ln(mix, w_out.astype(BF16), xf, row2(ln_g[i, 1]), row2(ln_b[i, 1]),
                         tm=OUT_TM, tk=OUT_TK)
        xf, xb = ffn(xb, xf, i, 1)
    return xf.reshape(batch, seq, d)
```

```python
import functools

import jax
import jax.numpy as jnp
from jax import lax
from jax.experimental import pallas as pl
from jax.experimental.pallas import tpu as pltpu

F32 = jnp.float32
BF16 = jnp.bfloat16

LOG2_E = 1.4426950408889634
LN_EPS = 1e-5
RMS_EPS = 1e-5
DEPTH = 2
DEEPNORM_ALPHA = (2.0 * DEPTH) ** 0.25

SB_HEADS = 16
SSD_HEAD_DIM = 64
SSD_GROUPS = 8
SSD_STATE = 128
SSD_CONV = 4

LANES = 128
SUBLANES = 8
VMEM_LIMIT = 56 * 1024 * 1024


def _params(sem):
    return pltpu.CompilerParams(dimension_semantics=sem, vmem_limit_bytes=VMEM_LIMIT)


def _layer_norm(r, g, b):
    mu = jnp.mean(r, axis=-1, keepdims=True)
    c = r - mu
    var = jnp.mean(c * c, axis=-1, keepdims=True)
    return c * lax.rsqrt(var + LN_EPS) * g + b


def _softplus(x):
    return jnp.maximum(x, 0.0) + jnp.log1p(jnp.exp(-jnp.abs(x)))


def _silu(x):
    return x * jax.nn.sigmoid(x)


def _split_bf16(x, parts):
    out = []
    for _ in range(parts - 1):
        p = x.astype(BF16)
        out.append(p)
        x = x - p.astype(F32)
    out.append(x.astype(BF16))
    return out


def _ffn_ln_kernel(xb_ref, xf_ref, wg_ref, wu_ref, wd_ref, g_ref, b_ref, yf_ref, yb_ref, acc_ref, *, nf):
    j = pl.program_id(1)

    @pl.when(j == 0)
    def _():
        acc_ref[...] = jnp.zeros_like(acc_ref)

    xb = xb_ref[...]
    gate = jnp.dot(xb, wg_ref[...], preferred_element_type=F32)
    up = jnp.dot(xb, wu_ref[...], preferred_element_type=F32)
    h = (_silu(gate) * up).astype(BF16)
    acc_ref[...] += jnp.dot(h, wd_ref[...], preferred_element_type=F32)

    @pl.when(j == nf - 1)
    def _():
        r = DEEPNORM_ALPHA * xf_ref[...] + 0.5 * acc_ref[...]
        y = _layer_norm(r, g_ref[...], b_ref[...])
        yf_ref[...] = y
        yb_ref[...] = y.astype(BF16)


def ffn_ln(xb, xf, w_gate_up, w_down, layer, slot, g, b, *, tm, tf):
    m, d = xf.shape
    f = w_down.shape[2]
    nf = f // tf
    assert m % tm == 0 and f % tf == 0 and w_gate_up.shape[2:] == (d, 2 * f)
    return pl.pallas_call(
        functools.partial(_ffn_ln_kernel, nf=nf),
        grid=(m // tm, nf),
        in_specs=[
            pl.BlockSpec((tm, d), lambda i, j: (i, 0)),
            pl.BlockSpec((tm, d), lambda i, j: (i, 0)),
            pl.BlockSpec((None, None, d, tf), lambda i, j: (layer, slot, 0, j)),
            pl.BlockSpec((None, None, d, tf), lambda i, j: (layer, slot, 0, j + nf)),
            pl.BlockSpec((None, None, tf, d), lambda i, j: (layer, slot, j, 0)),
            pl.BlockSpec((1, d), lambda i, j: (0, 0)),
            pl.BlockSpec((1, d), lambda i, j: (0, 0)),
        ],
        out_specs=[
            pl.BlockSpec((tm, d), lambda i, j: (i, 0)),
            pl.BlockSpec((tm, d), lambda i, j: (i, 0)),
        ],
        out_shape=[jax.ShapeDtypeStruct((m, d), F32), jax.ShapeDtypeStruct((m, d), BF16)],
        scratch_shapes=[pltpu.VMEM((tm, d), F32)],
        compiler_params=_params(("parallel", "arbitrary")),
        name="ffn_ln",
    )(xb, xf, w_gate_up, w_gate_up, w_down, g, b)


def _matmul_kernel(x_ref, w_ref, o_ref):
    o_ref[...] = jnp.dot(x_ref[...], w_ref[...], preferred_element_type=F32).astype(o_ref.dtype)


def matmul(x, w, out_dtype, *, tm, tn):
    m, k = x.shape
    n = w.shape[1]
    assert m % tm == 0 and n % tn == 0
    return pl.pallas_call(
        _matmul_kernel,
        grid=(n // tn, m // tm),
        in_specs=[
            pl.BlockSpec((tm, k), lambda j, i: (i, 0)),
            pl.BlockSpec((k, tn), lambda j, i: (0, j)),
        ],
        out_specs=pl.BlockSpec((tm, tn), lambda j, i: (i, j)),
        out_shape=jax.ShapeDtypeStruct((m, n), out_dtype),
        compiler_params=_params(("parallel", "parallel")),
        name="in_proj",
    )(x, w)


def _proj_ln_kernel(a_ref, w_ref, xf_ref, g_ref, b_ref, yf_ref, yb_ref, *, ts):
    for r0 in range(0, a_ref.shape[0], ts):
        rows = slice(r0, r0 + ts)
        mix = jnp.dot(a_ref[rows, :], w_ref[...], preferred_element_type=F32)
        y = _layer_norm(DEEPNORM_ALPHA * xf_ref[rows, :] + mix, g_ref[...], b_ref[...])
        yf_ref[rows, :] = y
        yb_ref[rows, :] = y.astype(BF16)


def proj_ln(a, w, xf, g, b, *, tm, ts):
    m, kdim = a.shape
    d = w.shape[1]
    assert m % tm == 0 and tm % ts == 0
    return pl.pallas_call(
        functools.partial(_proj_ln_kernel, ts=ts),
        grid=(m // tm,),
        in_specs=[
            pl.BlockSpec((tm, kdim), lambda i: (i, 0)),
            pl.BlockSpec((kdim, d), lambda i: (0, 0), pipeline_mode=pl.Buffered(1)),
            pl.BlockSpec((tm, d), lambda i: (i, 0)),
            pl.BlockSpec((1, d), lambda i: (0, 0)),
            pl.BlockSpec((1, d), lambda i: (0, 0)),
        ],
        out_specs=[
            pl.BlockSpec((tm, d), lambda i: (i, 0)),
            pl.BlockSpec((tm, d), lambda i: (i, 0)),
        ],
        out_shape=[jax.ShapeDtypeStruct((m, d), F32), jax.ShapeDtypeStruct((m, d), BF16)],
        compiler_params=_params(("parallel",)),
        name="proj_ln",
    )(a, w, xf, g, b)


def _sb_attn_kernel(q_ref, k_ref, v_ref, u_ref, o_ref, acc_ref, carry_ref, z_ref, *, tq, tk, dh, hb, scale):
    qi = pl.program_id(2)
    u2 = u_ref[...]
    acc_ref[...] = jnp.zeros_like(acc_ref)
    carry_ref[...] = jnp.zeros_like(carry_ref)
    n_diag = tq // tk
    cols = [slice(h * dh, (h + 1) * dh) for h in range(hb)]
    to_log2 = scale * LOG2_E

    def logits(kb, rows):
        start = pl.multiple_of(kb * tk, tk)
        return [lax.dot_general(q_ref[rows, c], k_ref[pl.ds(start, tk), c], (((1,), (1,)), ((), ())),
                                preferred_element_type=F32) * to_log2 for c in cols]

    def cum_log_keep(zs, strict):
        rs = []
        for z in zs:
            sp = jnp.maximum(z, 0.0) + jnp.log2(1.0 + jnp.exp2(-jnp.abs(z)))
            if strict is not None:
                sp = jnp.where(strict, sp, 0.0)
            hi, lo = _split_bf16(sp, 2)
            rs.append(jnp.dot(jnp.concatenate([hi, lo], axis=1), u2, preferred_element_type=F32))
        return rs

    def accumulate(kb, rows, zs, rs, strict):
        start = pl.multiple_of(kb * tk, tk)
        for c, z, r in zip(cols, zs, rs):
            carry = carry_ref[rows, c]
            att = jnp.exp2(z + r[:, :tk] + carry)
            if strict is not None:
                att = jnp.where(strict, att, 0.0)
            acc_ref[rows, c] += jnp.dot(att.astype(BF16), v_ref[pl.ds(start, tk), c],
                                        preferred_element_type=F32)
            carry_ref[rows, c] = carry + r[:, tk:]

    for j in reversed(range(n_diag)):
        kb = qi * n_diag + j
        rows = slice(j * tk, tq)
        q_pos = qi * tq + j * tk + lax.broadcasted_iota(jnp.int32, (tq - j * tk, tk), 0)
        k_pos = kb * tk + lax.broadcasted_iota(jnp.int32, (tq - j * tk, tk), 1)
        strict = k_pos < q_pos
        zs = logits(kb, rows)
        accumulate(kb, rows, zs, cum_log_keep(zs, strict), strict)

    n_full = qi * n_diag
    full = slice(0, tq)
    for c, z in zip(cols, logits(jnp.maximum(n_full - 1, 0), full)):
        z_ref[:, c] = z

    def body(i, carry_unused):
        kb = n_full - 1 - i
        zs = [z_ref[:, c] for c in cols]
        rs = cum_log_keep(zs, None)
        z_next = logits(jnp.maximum(kb - 1, 0), full)
        accumulate(kb, full, zs, rs, None)
        for c, z in zip(cols, z_next):
            z_ref[:, c] = z
        return carry_unused

    lax.fori_loop(0, n_full, body, 0)
    o_ref[...] = acc_ref[...].astype(o_ref.dtype)


def sb_attention(qkv, *, batch, seq, heads, tq, tk, hb):
    dh = qkv.shape[1] // (3 * heads)
    assert dh == tk == LANES and seq % tq == 0 and tq % tk == 0 and heads % hb == 0
    qkv3 = qkv.reshape(batch, seq, 3 * heads * dh)
    row = lax.broadcasted_iota(jnp.int32, (tk, 2 * tk), 0)
    col = lax.broadcasted_iota(jnp.int32, (tk, 2 * tk), 1)
    u = -((row >= col) | (col >= tk)).astype(BF16)
    u2 = jnp.concatenate([u, u], axis=0)
    nhb = heads // hb
    out = pl.pallas_call(
        functools.partial(_sb_attn_kernel, tq=tq, tk=tk, dh=dh, hb=hb, scale=dh ** -0.5),
        grid=(batch, nhb, seq // tq),
        in_specs=[
            pl.BlockSpec((None, tq, hb * dh), lambda b, h, i: (b, i, h)),
            pl.BlockSpec((None, seq, hb * dh), lambda b, h, i: (b, 0, nhb + h)),
            pl.BlockSpec((None, seq, hb * dh), lambda b, h, i: (b, 0, 2 * nhb + h)),
            pl.BlockSpec((2 * tk, 2 * tk), lambda b, h, i: (0, 0)),
        ],
        out_specs=pl.BlockSpec((None, tq, hb * dh), lambda b, h, i: (b, i, h)),
        out_shape=jax.ShapeDtypeStruct((batch, seq, heads * dh), BF16),
        scratch_shapes=[pltpu.VMEM((tq, hb * dh), F32), pltpu.VMEM((tq, hb * tk), F32),
                        pltpu.VMEM((tq, hb * tk), F32)],
        compiler_params=_params(("parallel", "parallel", "arbitrary")),
        name="sb_attn",
    )(qkv3, qkv3, qkv3, u2)
    return out.reshape(batch * seq, heads * dh)


def _ssd_kernel(z_ref, x_ref, bc_ref, dt_ref, cwx_ref, cbx_ref, cwbc_ref, cbbc_ref, dtb_ref, alog_ref,
                dskip_ref, ng_ref, y_ref, xtail_ref, bctail_ref, state_ref, yacc_ref, *, chunk, groups, hpg, p, n):
    c = pl.program_id(1)
    L = chunk
    S = SUBLANES
    K = SSD_CONV

    @pl.when(c == 0)
    def _():
        xtail_ref[...] = jnp.zeros_like(xtail_ref)
        bctail_ref[...] = jnp.zeros_like(bctail_ref)
        state_ref[...] = jnp.zeros_like(state_ref)

    def conv_silu(cur_ref, tail_ref, w_ref, b_ref):
        cur = cur_ref[...]
        first_row = lax.broadcasted_iota(jnp.int32, (S, cur.shape[1]), 0) == 0
        acc = w_ref[0:1, :] * cur
        for k in range(1, K):
            last = acc[L - 1:L, :]
            shifted = pltpu.roll(acc, 1, 0)
            head = jnp.where(first_row, tail_ref[k - 1:k, :], shifted[:S, :])
            tail_ref[k - 1:k, :] = last
            acc = jnp.concatenate([head, shifted[S:, :]], axis=0) + w_ref[k:k + 1, :] * cur
        return _silu(acc + b_ref[...])

    xs = conv_silu(x_ref, xtail_ref, cwx_ref, cbx_ref)
    bc = conv_silu(bc_ref, bctail_ref, cwbc_ref, cbbc_ref)

    dt = _softplus(dt_ref[...] + dtb_ref[...])
    a = -jnp.exp(alog_ref[...])
    row = lax.broadcasted_iota(jnp.int32, (L, L), 0)
    col = lax.broadcasted_iota(jnp.int32, (L, L), 1)
    causal = col <= row
    tri = causal.astype(BF16)
    a_cum = sum(jnp.dot(tri, part, preferred_element_type=F32) for part in _split_bf16(dt * a, 3))
    a_cum_t = a_cum.T
    dt_t = dt.T
    e_cum = jnp.exp(a_cum)
    w_end = jnp.exp(a_cum[L - 1:L, :] - a_cum) * dt
    lane_lo = lax.broadcasted_iota(jnp.int32, (L, 2 * p), 1) < p
    lane_lo_n = lax.broadcasted_iota(jnp.int32, (n, 2 * p), 1) < p

    for g in range(groups):
        bm = bc[:, g * n:(g + 1) * n]
        cm = bc[:, (groups + g) * n:(groups + g + 1) * n].astype(BF16)
        cb = lax.dot_general(cm, bm.astype(BF16), (((1,), (1,)), ((), ())), preferred_element_type=F32)
        bm_t = bm.T.astype(BF16)
        for pair in range(hpg // 2):
            h0 = g * hpg + 2 * pair
            lanes = slice(h0 * p, (h0 + 2) * p)
            x2 = xs[:, lanes]
            x2b = x2.astype(BF16)
            ys = []
            for h in (h0, h0 + 1):
                seg = a_cum[:, h:h + 1] - a_cum_t[h:h + 1, :]
                m = cb * jnp.exp(jnp.where(causal, seg, -jnp.inf)) * dt_t[h:h + 1, :]
                ys.append(jnp.dot(m.astype(BF16), x2b, preferred_element_type=F32))
            y_diag = jnp.where(lane_lo, ys[0], ys[1])
            prev = state_ref[:, lanes]
            e2 = jnp.where(lane_lo, e_cum[:, h0:h0 + 1], e_cum[:, h0 + 1:h0 + 2])
            y_off = jnp.dot(cm, prev.astype(BF16), preferred_element_type=F32) * e2
            w2 = jnp.where(lane_lo, w_end[:, h0:h0 + 1], w_end[:, h0 + 1:h0 + 2])
            st = jnp.dot(bm_t, (x2 * w2).astype(BF16), preferred_element_type=F32)
            dec = jnp.where(lane_lo_n, e_cum[L - 1:L, h0:h0 + 1], e_cum[L - 1:L, h0 + 1:h0 + 2])
            state_ref[:, lanes] = dec * prev + st
            yacc_ref[:, lanes] = y_diag + y_off + x2 * dskip_ref[:, lanes]

    gw = hpg * p
    for g in range(groups):
        lanes = slice(g * gw, (g + 1) * gw)
        yg = yacc_ref[:, lanes] * _silu(z_ref[:, lanes])
        ms = jnp.mean(yg * yg, axis=-1, keepdims=True)
        y_ref[:, lanes] = (yg * lax.rsqrt(ms + RMS_EPS) * ng_ref[:, lanes]).astype(y_ref.dtype)


def ssd_scan(proj, conv_w, conv_b, dt_bias, a_log, d_skip, norm_g, *, batch, seq, chunk, d_inner):
    groups, p, n = SSD_GROUPS, SSD_HEAD_DIM, SSD_STATE
    heads = d_inner // p
    hpg = heads // groups
    bcw = 2 * groups * n
    assert chunk == LANES and n == LANES and 2 * p == LANES and heads <= LANES and hpg % 2 == 0
    assert seq % chunk == 0 and d_inner % bcw == 0
    nc = seq // chunk
    pad = LANES - heads
    row2 = lambda v: v.reshape(1, -1)
    cwx, cwbc = conv_w[:, :d_inner], conv_w[:, d_inner:]
    cbx, cbbc = row2(conv_b[:d_inner]), row2(conv_b[d_inner:])
    dtb = row2(jnp.pad(dt_bias, (0, pad)))
    alog = row2(jnp.pad(a_log, (0, pad)))
    dskip = row2(jnp.repeat(d_skip, p))
    ng = row2(norm_g)
    x_blk = d_inner // d_inner
    bc_blk = 2 * d_inner // bcw
    dt_blk = (2 * d_inner + bcw) // LANES
    tok = lambda b, c: b * nc + c
    const = lambda shape: pl.BlockSpec(shape, lambda b, c: (0, 0))
    return pl.pallas_call(
        functools.partial(_ssd_kernel, chunk=chunk, groups=groups, hpg=hpg, p=p, n=n),
        grid=(batch, nc),
        in_specs=[
            pl.BlockSpec((chunk, d_inner), lambda b, c: (tok(b, c), 0)),
            pl.BlockSpec((chunk, d_inner), lambda b, c: (tok(b, c), x_blk)),
            pl.BlockSpec((chunk, bcw), lambda b, c: (tok(b, c), bc_blk)),
            pl.BlockSpec((chunk, LANES), lambda b, c: (tok(b, c), dt_blk)),
            const((SSD_CONV, d_inner)), const((1, d_inner)),
            const((SSD_CONV, bcw)), const((1, bcw)),
            const((1, LANES)), const((1, LANES)),
            const((1, d_inner)), const((1, d_inner)),
        ],
        out_specs=pl.BlockSpec((chunk, d_inner), lambda b, c: (tok(b, c), 0)),
        out_shape=jax.ShapeDtypeStruct((batch * seq, d_inner), BF16),
        scratch_shapes=[
            pltpu.VMEM((SUBLANES, d_inner), F32),
            pltpu.VMEM((SUBLANES, bcw), F32),
            pltpu.VMEM((n, d_inner), F32),
            pltpu.VMEM((chunk, d_inner), F32),
        ],
        compiler_params=_params(("parallel", "arbitrary")),
        name="ssd_scan",
    )(proj, proj, proj, proj, cwx, cbx, cwbc, cbbc, dtb, alog, dskip, ng)


FFN_TM, FFN_TF = 512, 512
PROJ_TM, PROJ_TN = 1024, 1024
OUT_TM, OUT_TS = 512, 128
SB_TQ, SB_TK, SB_HB = 256, 128, 8
SSD_CHUNK = 128
SSD_PROJ_TN = 1536


def kernel(x, ffn_w_gate_up, ffn_w_down, ln_g, ln_b, sb_w_in, sb_w_out, ssd_w_in, ssd_conv_w, ssd_conv_b,
           ssd_dt_bias, ssd_a_log, ssd_d, ssd_norm_g, ssd_w_out):
    batch, seq, d = x.shape
    m = batch * seq
    depth = ffn_w_gate_up.shape[0]
    assert depth == DEPTH
    xf = x.reshape(m, d)
    xb = xf.astype(BF16)
    row2 = lambda v: v.reshape(1, -1)

    w_gate_up, w_down = ffn_w_gate_up.astype(BF16), ffn_w_down.astype(BF16)

    def ffn(xb, xf, i, s):
        return ffn_ln(xb, xf, w_gate_up, w_down, i, s, row2(ln_g[i, 2 * s]), row2(ln_b[i, 2 * s]),
                      tm=FFN_TM, tf=FFN_TF)

    for i in range(depth):
        j = i // 2
        xf, xb = ffn(xb, xf, i, 0)
        if i % 2 == 0:
            qkv = matmul(xb, sb_w_in[j].astype(BF16), BF16, tm=PROJ_TM, tn=PROJ_TN)
            mix = sb_attention(qkv, batch=batch, seq=seq, heads=SB_HEADS, tq=SB_TQ, tk=SB_TK, hb=SB_HB)
            w_out = sb_w_out[j]
        else:
            d_inner = ssd_w_out.shape[1]
            w_in = ssd_w_in[j]
            n_pad = -w_in.shape[1] % SSD_PROJ_TN
            w_in = jnp.pad(w_in, ((0, 0), (0, n_pad))).astype(BF16)
            proj = matmul(xb, w_in, F32, tm=PROJ_TM, tn=SSD_PROJ_TN)
            mix = ssd_scan(proj, ssd_conv_w[j], ssd_conv_b[j], ssd_dt_bias[j], ssd_a_log[j], ssd_d[j],
                           ssd_norm_g[j], batch=batch, seq=seq, chunk=SSD_CHUNK, d_inner=d_inner)
            w_out = ssd_w_out[j]
        xf, xb = proj_ln(mix, w_out.astype(BF16), xf, row2(ln_g[i, 1]), row2(ln_b[i, 1]),
                         tm=OUT_TM, ts=OUT_TS)
        xf, xb = ffn(xb, xf, i, 1)
    return xf.reshape(batch, seq, d)
```

```python
import functools

import jax
import jax.numpy as jnp
from jax import lax
from jax.experimental import pallas as pl
from jax.experimental.pallas import tpu as pltpu

F32 = jnp.float32
BF16 = jnp.bfloat16

LOG2_E = 1.4426950408889634
SIGN_BIT = 0x80000000
BF16_BITS = 0xFFFF0000
LN_EPS = 1e-5
RMS_EPS = 1e-5
DEPTH = 2
DEEPNORM_ALPHA = (2.0 * DEPTH) ** 0.25

SB_HEADS = 16
SSD_HEAD_DIM = 64
SSD_GROUPS = 8
SSD_STATE = 128
SSD_CONV = 4

LANES = 128
SUBLANES = 8
VMEM_LIMIT = 56 * 1024 * 1024


def _params(sem):
    return pltpu.CompilerParams(dimension_semantics=sem, vmem_limit_bytes=VMEM_LIMIT)


def _layer_norm(r, g, b):
    mu = jnp.mean(r, axis=-1, keepdims=True)
    c = r - mu
    var = jnp.mean(c * c, axis=-1, keepdims=True)
    return c * lax.rsqrt(var + LN_EPS) * g + b


def _softplus(x):
    return jnp.maximum(x, 0.0) + jnp.log1p(jnp.exp(-jnp.abs(x)))


def _silu(x):
    return x * jax.nn.sigmoid(x)


def _split_bf16(x, parts):
    out = []
    for _ in range(parts - 1):
        p = x.astype(BF16)
        out.append(p)
        x = x - p.astype(F32)
    out.append(x.astype(BF16))
    return out


def _ffn_ln_kernel(xb_ref, xf_hbm, wg_ref, wu_ref, wd_ref, g_ref, b_ref, yf_hbm, yb_ref,
                   acc_ref, res_ref, sem_in, sem_out, *, nm, nf, ts, tl):
    i, j = pl.program_id(0), pl.program_id(1)
    tm = acc_ref.shape[0]

    def tile_rows(t):
        return pl.ds(pl.multiple_of(t * tm, tm), tm)

    def fetch_residual():
        return pltpu.make_async_copy(xf_hbm.at[tile_rows(i), :], res_ref, sem_in)

    def write_back(t):
        return pltpu.make_async_copy(res_ref, yf_hbm.at[tile_rows(t), :], sem_out)

    @pl.when(j == 0)
    def _():
        acc_ref[...] = jnp.zeros_like(acc_ref)

    @pl.when(j == nf - 2)
    def _():
        @pl.when(i > 0)
        def _():
            write_back(i - 1).wait()

        fetch_residual().start()

    for r0 in range(0, tm, ts):
        rows = slice(r0, r0 + ts)
        xb = xb_ref[rows, :]
        gate = jnp.dot(xb, wg_ref[...], preferred_element_type=F32)
        up = jnp.dot(xb, wu_ref[...], preferred_element_type=F32)
        h = (_silu(gate) * up).astype(BF16)
        acc_ref[rows, :] += jnp.dot(h, wd_ref[...], preferred_element_type=F32)

    @pl.when(j == nf - 1)
    def _():
        fetch_residual().wait()
        for r0 in range(0, tm, tl):
            rows = slice(r0, r0 + tl)
            y = _layer_norm(DEEPNORM_ALPHA * res_ref[rows, :] + 0.5 * acc_ref[rows, :], g_ref[...], b_ref[...])
            res_ref[rows, :] = y
            yb_ref[rows, :] = y.astype(BF16)
        write_back(i).start()

        @pl.when(i == nm - 1)
        def _():
            write_back(i).wait()


def ffn_ln(xb, xf, w_gate_up, w_down, layer, slot, g, b, *, tm, tf, ts, tl):
    m, d = xf.shape
    f = w_down.shape[2]
    nm, nf = m // tm, f // tf
    assert m % tm == 0 and f % tf == 0 and nf >= 2 and tm % ts == 0 and tm % tl == 0
    assert w_gate_up.shape[2:] == (d, 2 * f)
    return pl.pallas_call(
        functools.partial(_ffn_ln_kernel, nm=nm, nf=nf, ts=ts, tl=tl),
        grid=(nm, nf),
        in_specs=[
            pl.BlockSpec((tm, d), lambda i, j: (i, 0)),
            pl.BlockSpec(memory_space=pl.ANY),
            pl.BlockSpec((None, None, d, tf), lambda i, j: (layer, slot, 0, j)),
            pl.BlockSpec((None, None, d, tf), lambda i, j: (layer, slot, 0, j + nf)),
            pl.BlockSpec((None, None, tf, d), lambda i, j: (layer, slot, j, 0)),
            pl.BlockSpec((1, d), lambda i, j: (0, 0)),
            pl.BlockSpec((1, d), lambda i, j: (0, 0)),
        ],
        out_specs=[
            pl.BlockSpec(memory_space=pl.ANY),
            pl.BlockSpec((tm, d), lambda i, j: (i, 0)),
        ],
        out_shape=[jax.ShapeDtypeStruct((m, d), F32), jax.ShapeDtypeStruct((m, d), BF16)],
        scratch_shapes=[pltpu.VMEM((tm, d), F32), pltpu.VMEM((tm, d), F32),
                        pltpu.SemaphoreType.DMA(()), pltpu.SemaphoreType.DMA(())],
        compiler_params=_params(("arbitrary", "arbitrary")),
        name="ffn_ln",
    )(xb, xf, w_gate_up, w_gate_up, w_down, g, b)


def _matmul_kernel(x_ref, w_ref, o_ref):
    o_ref[...] = jnp.dot(x_ref[...], w_ref[...], preferred_element_type=F32).astype(o_ref.dtype)


def matmul(x, w, out_dtype, *, tm, tn):
    m, k = x.shape
    n = w.shape[1]
    assert m % tm == 0 and n % tn == 0
    return pl.pallas_call(
        _matmul_kernel,
        grid=(n // tn, m // tm),
        in_specs=[
            pl.BlockSpec((tm, k), lambda j, i: (i, 0)),
            pl.BlockSpec((k, tn), lambda j, i: (0, j)),
        ],
        out_specs=pl.BlockSpec((tm, tn), lambda j, i: (i, j)),
        out_shape=jax.ShapeDtypeStruct((m, n), out_dtype),
        compiler_params=_params(("parallel", "parallel")),
        name="in_proj",
    )(x, w)


def _proj_ln_kernel(a_ref, w_ref, xf_ref, g_ref, b_ref, yf_ref, yb_ref, *, ts):
    for r0 in range(0, a_ref.shape[0], ts):
        rows = slice(r0, r0 + ts)
        mix = jnp.dot(a_ref[rows, :], w_ref[...], preferred_element_type=F32)
        y = _layer_norm(DEEPNORM_ALPHA * xf_ref[rows, :] + mix, g_ref[...], b_ref[...])
        yf_ref[rows, :] = y
        yb_ref[rows, :] = y.astype(BF16)


def proj_ln(a, w, xf, g, b, *, tm, ts):
    m, kdim = a.shape
    d = w.shape[1]
    assert m % tm == 0 and tm % ts == 0
    return pl.pallas_call(
        functools.partial(_proj_ln_kernel, ts=ts),
        grid=(m // tm,),
        in_specs=[
            pl.BlockSpec((tm, kdim), lambda i: (i, 0)),
            pl.BlockSpec((kdim, d), lambda i: (0, 0), pipeline_mode=pl.Buffered(1)),
            pl.BlockSpec((tm, d), lambda i: (i, 0)),
            pl.BlockSpec((1, d), lambda i: (0, 0)),
            pl.BlockSpec((1, d), lambda i: (0, 0)),
        ],
        out_specs=[
            pl.BlockSpec((tm, d), lambda i: (i, 0)),
            pl.BlockSpec((tm, d), lambda i: (i, 0)),
        ],
        out_shape=[jax.ShapeDtypeStruct((m, d), F32), jax.ShapeDtypeStruct((m, d), BF16)],
        compiler_params=_params(("parallel",)),
        name="proj_ln",
    )(a, w, xf, g, b)


def _sb_attn_kernel(q_ref, k_ref, v_ref, u_ref, o_ref, acc_ref, carry_ref, z_ref, *, tq, tk, dh, hb, scale):
    qi = pl.program_id(2)
    u2 = u_ref[...]
    acc_ref[...] = jnp.zeros_like(acc_ref)
    carry_ref[...] = jnp.zeros_like(carry_ref)
    n_diag = tq // tk
    cols = [slice(h * dh, (h + 1) * dh) for h in range(hb)]
    to_log2 = scale * LOG2_E

    def logits(kb, rows):
        start = pl.multiple_of(kb * tk, tk)
        return [lax.dot_general(q_ref[rows, c], k_ref[pl.ds(start, tk), c], (((1,), (1,)), ((), ())),
                                preferred_element_type=F32) * to_log2 for c in cols]

    def cum_log_keep(zs, strict):
        rs = []
        for z in zs:
            neg_abs = lax.bitcast_convert_type(lax.bitcast_convert_type(z, jnp.uint32) | jnp.uint32(SIGN_BIT), F32)
            sp = jnp.maximum(z, 0.0) + jnp.log2(1.0 + jnp.exp2(neg_abs))
            if strict is not None:
                sp = jnp.where(strict, sp, 0.0)
            hi32 = lax.bitcast_convert_type(lax.bitcast_convert_type(sp, jnp.uint32) & jnp.uint32(BF16_BITS), F32)
            hi, lo = hi32.astype(BF16), (sp - hi32).astype(BF16)
            rs.append(jnp.dot(jnp.concatenate([hi, lo], axis=1), u2, preferred_element_type=F32))
        return rs

    def accumulate(kb, rows, zs, rs, strict):
        start = pl.multiple_of(kb * tk, tk)
        for c, z, r in zip(cols, zs, rs):
            carry = carry_ref[rows, c]
            att = jnp.exp2(z + r[:, :tk] + carry)
            if strict is not None:
                att = jnp.where(strict, att, 0.0)
            acc_ref[rows, c] += jnp.dot(att.astype(BF16), v_ref[pl.ds(start, tk), c],
                                        preferred_element_type=F32)
            carry_ref[rows, c] = carry + r[:, tk:]

    for j in reversed(range(n_diag)):
        kb = qi * n_diag + j
        rows = slice(j * tk, tq)
        q_pos = qi * tq + j * tk + lax.broadcasted_iota(jnp.int32, (tq - j * tk, tk), 0)
        k_pos = kb * tk + lax.broadcasted_iota(jnp.int32, (tq - j * tk, tk), 1)
        strict = k_pos < q_pos
        zs = logits(kb, rows)
        accumulate(kb, rows, zs, cum_log_keep(zs, strict), strict)

    n_full = qi * n_diag
    full = slice(0, tq)
    for c, z in zip(cols, logits(jnp.maximum(n_full - 1, 0), full)):
        z_ref[:, c] = z

    def body(i, carry_unused):
        kb = n_full - 1 - i
        zs = [z_ref[:, c] for c in cols]
        rs = cum_log_keep(zs, None)
        z_next = logits(jnp.maximum(kb - 1, 0), full)
        accumulate(kb, full, zs, rs, None)
        for c, z in zip(cols, z_next):
            z_ref[:, c] = z
        return carry_unused

    lax.fori_loop(0, n_full, body, 0)
    o_ref[...] = acc_ref[...].astype(o_ref.dtype)


def sb_attention(qkv, *, batch, seq, heads, tq, tk, hb):
    dh = qkv.shape[1] // (3 * heads)
    assert dh == tk == LANES and seq % tq == 0 and tq % tk == 0 and heads % hb == 0
    qkv3 = qkv.reshape(batch, seq, 3 * heads * dh)
    row = lax.broadcasted_iota(jnp.int32, (tk, 2 * tk), 0)
    col = lax.broadcasted_iota(jnp.int32, (tk, 2 * tk), 1)
    u = -((row >= col) | (col >= tk)).astype(BF16)
    u2 = jnp.concatenate([u, u], axis=0)
    nhb = heads // hb
    out = pl.pallas_call(
        functools.partial(_sb_attn_kernel, tq=tq, tk=tk, dh=dh, hb=hb, scale=dh ** -0.5),
        grid=(batch, nhb, seq // tq),
        in_specs=[
            pl.BlockSpec((None, tq, hb * dh), lambda b, h, i: (b, i, h)),
            pl.BlockSpec((None, seq, hb * dh), lambda b, h, i: (b, 0, nhb + h)),
            pl.BlockSpec((None, seq, hb * dh), lambda b, h, i: (b, 0, 2 * nhb + h)),
            pl.BlockSpec((2 * tk, 2 * tk), lambda b, h, i: (0, 0)),
        ],
        out_specs=pl.BlockSpec((None, tq, hb * dh), lambda b, h, i: (b, i, h)),
        out_shape=jax.ShapeDtypeStruct((batch, seq, heads * dh), BF16),
        scratch_shapes=[pltpu.VMEM((tq, hb * dh), F32), pltpu.VMEM((tq, hb * tk), F32),
                        pltpu.VMEM((tq, hb * tk), F32)],
        compiler_params=_params(("parallel", "parallel", "arbitrary")),
        name="sb_attn",
    )(qkv3, qkv3, qkv3, u2)
    return out.reshape(batch * seq, heads * dh)


def _ssd_kernel(z_ref, x_ref, bc_ref, dt_ref, cwx_ref, cbx_ref, cwbc_ref, cbbc_ref, dtb_ref, alog_ref,
                dskip_ref, ng_ref, y_ref, xtail_ref, bctail_ref, state_ref, yacc_ref, *, chunk, groups, hpg, p, n):
    c = pl.program_id(1)
    L = chunk
    S = SUBLANES
    K = SSD_CONV

    @pl.when(c == 0)
    def _():
        xtail_ref[...] = jnp.zeros_like(xtail_ref)
        bctail_ref[...] = jnp.zeros_like(bctail_ref)
        state_ref[...] = jnp.zeros_like(state_ref)

    def conv_silu(cur_ref, tail_ref, w_ref, b_ref):
        cur = cur_ref[...]
        first_row = lax.broadcasted_iota(jnp.int32, (S, cur.shape[1]), 0) == 0
        acc = w_ref[0:1, :] * cur
        for k in range(1, K):
            last = acc[L - 1:L, :]
            shifted = pltpu.roll(acc, 1, 0)
            head = jnp.where(first_row, tail_ref[k - 1:k, :], shifted[:S, :])
            tail_ref[k - 1:k, :] = last
            acc = jnp.concatenate([head, shifted[S:, :]], axis=0) + w_ref[k:k + 1, :] * cur
        return _silu(acc + b_ref[...])

    xs = conv_silu(x_ref, xtail_ref, cwx_ref, cbx_ref)
    bc = conv_silu(bc_ref, bctail_ref, cwbc_ref, cbbc_ref)

    dt = _softplus(dt_ref[...] + dtb_ref[...])
    a = -jnp.exp(alog_ref[...]) * LOG2_E
    row = lax.broadcasted_iota(jnp.int32, (L, L), 0)
    col = lax.broadcasted_iota(jnp.int32, (L, L), 1)
    causal = col <= row
    tri = causal.astype(BF16)
    a_cum = sum(jnp.dot(tri, part, preferred_element_type=F32) for part in _split_bf16(dt * a, 3))
    a_cum_t = a_cum.T
    dt_t = dt.T
    e_cum = jnp.exp2(a_cum)
    w_end = jnp.exp2(a_cum[L - 1:L, :] - a_cum) * dt
    lane_lo = lax.broadcasted_iota(jnp.int32, (L, 2 * p), 1) < p
    lane_lo_n = lax.broadcasted_iota(jnp.int32, (n, 2 * p), 1) < p

    for g in range(groups):
        bm = bc[:, g * n:(g + 1) * n]
        cm = bc[:, (groups + g) * n:(groups + g + 1) * n].astype(BF16)
        cb = lax.dot_general(cm, bm.astype(BF16), (((1,), (1,)), ((), ())), preferred_element_type=F32)
        bm_t = bm.T.astype(BF16)
        for pair in range(hpg // 2):
            h0 = g * hpg + 2 * pair
            lanes = slice(h0 * p, (h0 + 2) * p)
            x2 = xs[:, lanes]
            x2b = x2.astype(BF16)
            ys = []
            for h in (h0, h0 + 1):
                seg = a_cum[:, h:h + 1] - a_cum_t[h:h + 1, :]
                m = cb * jnp.exp2(jnp.where(causal, seg, -jnp.inf)) * dt_t[h:h + 1, :]
                ys.append(jnp.dot(m.astype(BF16), x2b, preferred_element_type=F32))
            y_diag = jnp.where(lane_lo, ys[0], ys[1])
            prev = state_ref[:, lanes]
            e2 = jnp.where(lane_lo, e_cum[:, h0:h0 + 1], e_cum[:, h0 + 1:h0 + 2])
            y_off = jnp.dot(cm, prev.astype(BF16), preferred_element_type=F32) * e2
            w2 = jnp.where(lane_lo, w_end[:, h0:h0 + 1], w_end[:, h0 + 1:h0 + 2])
            st = jnp.dot(bm_t, (x2 * w2).astype(BF16), preferred_element_type=F32)
            dec = jnp.where(lane_lo_n, e_cum[L - 1:L, h0:h0 + 1], e_cum[L - 1:L, h0 + 1:h0 + 2])
            state_ref[:, lanes] = dec * prev + st
            yacc_ref[:, lanes] = y_diag + y_off + x2 * dskip_ref[:, lanes]

    gw = hpg * p
    for g in range(groups):
        lanes = slice(g * gw, (g + 1) * gw)
        yg = yacc_ref[:, lanes] * _silu(z_ref[:, lanes])
        ms = jnp.mean(yg * yg, axis=-1, keepdims=True)
        y_ref[:, lanes] = (yg * lax.rsqrt(ms + RMS_EPS) * ng_ref[:, lanes]).astype(y_ref.dtype)


def ssd_scan(proj, dt_raw, conv_w, conv_b, dt_bias, a_log, d_skip, norm_g, *, batch, seq, chunk, d_inner):
    groups, p, n = SSD_GROUPS, SSD_HEAD_DIM, SSD_STATE
    heads = d_inner // p
    hpg = heads // groups
    bcw = 2 * groups * n
    assert chunk == LANES and n == LANES and 2 * p == LANES and heads <= LANES and hpg % 2 == 0
    assert seq % chunk == 0 and d_inner % bcw == 0
    nc = seq // chunk
    pad = LANES - heads
    row2 = lambda v: v.reshape(1, -1)
    cwx, cwbc = conv_w[:, :d_inner], conv_w[:, d_inner:]
    cbx, cbbc = row2(conv_b[:d_inner]), row2(conv_b[d_inner:])
    dtb = row2(jnp.pad(dt_bias, (0, pad)))
    alog = row2(jnp.pad(a_log, (0, pad)))
    dskip = row2(jnp.repeat(d_skip, p))
    ng = row2(norm_g)
    x_blk = d_inner // d_inner
    bc_blk = 2 * d_inner // bcw
    tok = lambda b, c: b * nc + c
    const = lambda shape: pl.BlockSpec(shape, lambda b, c: (0, 0))
    return pl.pallas_call(
        functools.partial(_ssd_kernel, chunk=chunk, groups=groups, hpg=hpg, p=p, n=n),
        grid=(batch, nc),
        in_specs=[
            pl.BlockSpec((chunk, d_inner), lambda b, c: (tok(b, c), 0)),
            pl.BlockSpec((chunk, d_inner), lambda b, c: (tok(b, c), x_blk)),
            pl.BlockSpec((chunk, bcw), lambda b, c: (tok(b, c), bc_blk)),
            pl.BlockSpec((chunk, LANES), lambda b, c: (tok(b, c), 0)),
            const((SSD_CONV, d_inner)), const((1, d_inner)),
            const((SSD_CONV, bcw)), const((1, bcw)),
            const((1, LANES)), const((1, LANES)),
            const((1, d_inner)), const((1, d_inner)),
        ],
        out_specs=pl.BlockSpec((chunk, d_inner), lambda b, c: (tok(b, c), 0)),
        out_shape=jax.ShapeDtypeStruct((batch * seq, d_inner), BF16),
        scratch_shapes=[
            pltpu.VMEM((SUBLANES, d_inner), F32),
            pltpu.VMEM((SUBLANES, bcw), F32),
            pltpu.VMEM((n, d_inner), F32),
            pltpu.VMEM((chunk, d_inner), F32),
        ],
        compiler_params=_params(("parallel", "arbitrary")),
        name="ssd_scan",
    )(proj, proj, proj, dt_raw, cwx, cbx, cwbc, cbbc, dtb, alog, dskip, ng)


FFN_TM, FFN_TF, FFN_TS, FFN_TL = 1024, 512, 512, 256
PROJ_TM, PROJ_TN = 1024, 1024
OUT_TM, OUT_TS = 512, 128
SB_TQ, SB_TK, SB_HB = 256, 128, 8
SSD_CHUNK = 128
SSD_PROJ_TN = 2048


def kernel(x, ffn_w_gate_up, ffn_w_down, ln_g, ln_b, sb_w_in, sb_w_out, ssd_w_in, ssd_conv_w, ssd_conv_b,
           ssd_dt_bias, ssd_a_log, ssd_d, ssd_norm_g, ssd_w_out):
    batch, seq, d = x.shape
    m = batch * seq
    depth = ffn_w_gate_up.shape[0]
    assert depth == DEPTH
    xf = x.reshape(m, d)
    xb = xf.astype(BF16)
    row2 = lambda v: v.reshape(1, -1)

    w_gate_up, w_down = ffn_w_gate_up.astype(BF16), ffn_w_down.astype(BF16)

    def ffn(xb, xf, i, s):
        return ffn_ln(xb, xf, w_gate_up, w_down, i, s, row2(ln_g[i, 2 * s]), row2(ln_b[i, 2 * s]),
                      tm=FFN_TM, tf=FFN_TF, ts=FFN_TS, tl=FFN_TL)

    for i in range(depth):
        j = i // 2
        xf, xb = ffn(xb, xf, i, 0)
        if i % 2 == 0:
            qkv = matmul(xb, sb_w_in[j].astype(BF16), BF16, tm=PROJ_TM, tn=PROJ_TN)
            mix = sb_attention(qkv, batch=batch, seq=seq, heads=SB_HEADS, tq=SB_TQ, tk=SB_TK, hb=SB_HB)
            w_out = sb_w_out[j]
        else:
            d_inner = ssd_w_out.shape[1]
            n_main = 2 * d_inner + 2 * SSD_GROUPS * SSD_STATE
            proj = matmul(xb, ssd_w_in[j, :, :n_main].astype(BF16), F32, tm=PROJ_TM, tn=SSD_PROJ_TN)
            w_dt = jnp.pad(ssd_w_in[j, :, n_main:], ((0, 0), (0, LANES - ssd_dt_bias.shape[1]))).astype(BF16)
            dt_raw = matmul(xb, w_dt, F32, tm=PROJ_TM, tn=LANES)
            mix = ssd_scan(proj, dt_raw, ssd_conv_w[j], ssd_conv_b[j],
                           ssd_dt_bias[j], ssd_a_log[j], ssd_d[j], ssd_norm_g[j],
                           batch=batch, seq=seq, chunk=SSD_CHUNK, d_inner=d_inner)
            w_out = ssd_w_out[j]
        xf, xb = proj_ln(mix, w_out.astype(BF16), xf, row2(ln_g[i, 1]), row2(ln_b[i, 1]),
                         tm=OUT_TM, ts=OUT_TS)
        xf, xb = ffn(xb, xf, i, 1)
    return xf.reshape(batch, seq, d)
```

```python
import functools

import jax
import jax.numpy as jnp
from jax import lax
from jax.experimental import pallas as pl
from jax.experimental.pallas import tpu as pltpu

F32 = jnp.float32
BF16 = jnp.bfloat16

LOG2_E = 1.4426950408889634
SIGN_BIT = 0x80000000
BF16_BITS = 0xFFFF0000
LN_EPS = 1e-5
RMS_EPS = 1e-5
DEPTH = 2
DEEPNORM_ALPHA = (2.0 * DEPTH) ** 0.25

SB_HEADS = 16
SSD_HEAD_DIM = 64
SSD_GROUPS = 8
SSD_STATE = 128
SSD_CONV = 4

LANES = 128
SUBLANES = 8
VMEM_LIMIT = 56 * 1024 * 1024


def _params(sem):
    return pltpu.CompilerParams(dimension_semantics=sem, vmem_limit_bytes=VMEM_LIMIT)


def _layer_norm(r, g, b):
    mu = jnp.mean(r, axis=-1, keepdims=True)
    c = r - mu
    var = jnp.mean(c * c, axis=-1, keepdims=True)
    return c * lax.rsqrt(var + LN_EPS) * g + b


def _softplus(x):
    return jnp.maximum(x, 0.0) + jnp.log1p(jnp.exp(-jnp.abs(x)))


def _silu(x):
    return x * jax.nn.sigmoid(x)


def _split_bf16(x, parts):
    out = []
    for _ in range(parts - 1):
        p = x.astype(BF16)
        out.append(p)
        x = x - p.astype(F32)
    out.append(x.astype(BF16))
    return out


def _ffn_ln_kernel(*refs, nm, nf, ts, tl, n_casts):
    xb_ref, xf_hbm, wg_ref, wu_ref, wd_ref, g_ref, b_ref = refs[:7]
    cast_in = refs[7:7 + n_casts]
    yf_hbm, yb_ref = refs[7 + n_casts:9 + n_casts]
    cast_out = refs[9 + n_casts:9 + 2 * n_casts]
    acc_ref, res_ref, sem_in, sem_out = refs[9 + 2 * n_casts:]
    for src, dst in zip(cast_in, cast_out):
        dst[...] = src[...].astype(BF16)
    i, j = pl.program_id(0), pl.program_id(1)
    tm = acc_ref.shape[0]

    def tile_rows(t):
        return pl.ds(pl.multiple_of(t * tm, tm), tm)

    def fetch_residual():
        return pltpu.make_async_copy(xf_hbm.at[tile_rows(i), :], res_ref, sem_in)

    def write_back(t):
        return pltpu.make_async_copy(res_ref, yf_hbm.at[tile_rows(t), :], sem_out)

    @pl.when(j == 0)
    def _():
        acc_ref[...] = jnp.zeros_like(acc_ref)

    @pl.when(j == nf - 2)
    def _():
        @pl.when(i > 0)
        def _():
            write_back(i - 1).wait()

        fetch_residual().start()

    for r0 in range(0, tm, ts):
        rows = slice(r0, r0 + ts)
        xb = xb_ref[rows, :]
        gate = jnp.dot(xb, wg_ref[...], preferred_element_type=F32)
        up = jnp.dot(xb, wu_ref[...], preferred_element_type=F32)
        h = (_silu(gate) * up).astype(BF16)
        acc_ref[rows, :] += jnp.dot(h, wd_ref[...], preferred_element_type=F32)

    @pl.when(j == nf - 1)
    def _():
        fetch_residual().wait()
        for r0 in range(0, tm, tl):
            rows = slice(r0, r0 + tl)
            y = _layer_norm(DEEPNORM_ALPHA * res_ref[rows, :] + 0.5 * acc_ref[rows, :], g_ref[...], b_ref[...])
            res_ref[rows, :] = y
            yb_ref[rows, :] = y.astype(BF16)
        write_back(i).start()

        @pl.when(i == nm - 1)
        def _():
            write_back(i).wait()


def cast_job(w, lead, n_row_blocks, n_col_blocks, cols=None, col_major_grid=False):
    rows, width = w.shape[len(lead):]
    cols = width if cols is None else cols
    assert rows % n_row_blocks == 0 and cols % n_col_blocks == 0
    blk = (rows // n_row_blocks, cols // n_col_blocks)
    assert blk[0] % (2 * SUBLANES) == 0 and blk[1] % LANES == 0

    def block_index(i, j):
        r, c = (j, i) if col_major_grid else (i, j)
        return jnp.minimum(r, n_row_blocks - 1), jnp.minimum(c, n_col_blocks - 1)

    in_spec = pl.BlockSpec((None,) * len(lead) + blk, lambda i, j: tuple(lead) + block_index(i, j))
    out_spec = pl.BlockSpec(blk, block_index)
    return w, in_spec, out_spec, jax.ShapeDtypeStruct((rows, cols), BF16)


def ffn_ln(xb, xf, w_gate_up, w_down, g, b, casts, *, tm, tf, ts, tl):
    m, d = xf.shape
    f = w_down.shape[0]
    nm, nf = m // tm, f // tf
    assert m % tm == 0 and f % tf == 0 and nf >= 2 and tm % ts == 0 and tm % tl == 0
    assert w_gate_up.shape == (d, 2 * f)
    return pl.pallas_call(
        functools.partial(_ffn_ln_kernel, nm=nm, nf=nf, ts=ts, tl=tl, n_casts=len(casts)),
        grid=(nm, nf),
        in_specs=[
            pl.BlockSpec((tm, d), lambda i, j: (i, 0)),
            pl.BlockSpec(memory_space=pl.ANY),
            pl.BlockSpec((d, tf), lambda i, j: (0, j)),
            pl.BlockSpec((d, tf), lambda i, j: (0, j + nf)),
            pl.BlockSpec((tf, d), lambda i, j: (j, 0)),
            pl.BlockSpec((1, d), lambda i, j: (0, 0)),
            pl.BlockSpec((1, d), lambda i, j: (0, 0)),
        ] + [c[1] for c in casts],
        out_specs=[
            pl.BlockSpec(memory_space=pl.ANY),
            pl.BlockSpec((tm, d), lambda i, j: (i, 0)),
        ] + [c[2] for c in casts],
        out_shape=[jax.ShapeDtypeStruct((m, d), F32), jax.ShapeDtypeStruct((m, d), BF16)] + [c[3] for c in casts],
        scratch_shapes=[pltpu.VMEM((tm, d), F32), pltpu.VMEM((tm, d), F32),
                        pltpu.SemaphoreType.DMA(()), pltpu.SemaphoreType.DMA(())],
        compiler_params=_params(("arbitrary", "arbitrary")),
        name="ffn_ln",
    )(xb, xf, w_gate_up, w_gate_up, w_down, g, b, *[c[0] for c in casts])


def _matmul_kernel(x_ref, w_ref, o_ref):
    o_ref[...] = jnp.dot(x_ref[...], w_ref[...], preferred_element_type=F32).astype(o_ref.dtype)


def matmul(x, w, out_dtype, *, tm, tn):
    m, k = x.shape
    n = w.shape[1]
    assert m % tm == 0 and n % tn == 0
    return pl.pallas_call(
        _matmul_kernel,
        grid=(n // tn, m // tm),
        in_specs=[
            pl.BlockSpec((tm, k), lambda j, i: (i, 0)),
            pl.BlockSpec((k, tn), lambda j, i: (0, j)),
        ],
        out_specs=pl.BlockSpec((tm, tn), lambda j, i: (i, j)),
        out_shape=jax.ShapeDtypeStruct((m, n), out_dtype),
        compiler_params=_params(("parallel", "parallel")),
        name="in_proj",
    )(x, w)


def _proj_ln_kernel(a_ref, w_ref, xf_ref, g_ref, b_ref, yf_ref, yb_ref, *, ts):
    for r0 in range(0, a_ref.shape[0], ts):
        rows = slice(r0, r0 + ts)
        mix = jnp.dot(a_ref[rows, :], w_ref[...], preferred_element_type=F32)
        y = _layer_norm(DEEPNORM_ALPHA * xf_ref[rows, :] + mix, g_ref[...], b_ref[...])
        yf_ref[rows, :] = y
        yb_ref[rows, :] = y.astype(BF16)


def proj_ln(a, w, xf, g, b, *, tm, ts):
    m, kdim = a.shape
    d = w.shape[1]
    assert m % tm == 0 and tm % ts == 0
    return pl.pallas_call(
        functools.partial(_proj_ln_kernel, ts=ts),
        grid=(m // tm,),
        in_specs=[
            pl.BlockSpec((tm, kdim), lambda i: (i, 0)),
            pl.BlockSpec((kdim, d), lambda i: (0, 0), pipeline_mode=pl.Buffered(1)),
            pl.BlockSpec((tm, d), lambda i: (i, 0)),
            pl.BlockSpec((1, d), lambda i: (0, 0)),
            pl.BlockSpec((1, d), lambda i: (0, 0)),
        ],
        out_specs=[
            pl.BlockSpec((tm, d), lambda i: (i, 0)),
            pl.BlockSpec((tm, d), lambda i: (i, 0)),
        ],
        out_shape=[jax.ShapeDtypeStruct((m, d), F32), jax.ShapeDtypeStruct((m, d), BF16)],
        compiler_params=_params(("parallel",)),
        name="proj_ln",
    )(a, w, xf, g, b)


def _sb_attn_kernel(q_ref, k_ref, v_ref, u_ref, o_ref, acc_ref, carry_ref, z_ref, *, tq, tk, dh, hb, scale):
    qi = pl.program_id(2)
    u2 = u_ref[...]
    acc_ref[...] = jnp.zeros_like(acc_ref)
    carry_ref[...] = jnp.zeros_like(carry_ref)
    n_diag = tq // tk
    cols = [slice(h * dh, (h + 1) * dh) for h in range(hb)]
    to_log2 = scale * LOG2_E

    def logits(kb, rows):
        start = pl.multiple_of(kb * tk, tk)
        return [lax.dot_general(q_ref[rows, c], k_ref[pl.ds(start, tk), c], (((1,), (1,)), ((), ())),
                                preferred_element_type=F32) * to_log2 for c in cols]

    def cum_log_keep(zs, strict):
        rs = []
        for z in zs:
            neg_abs = lax.bitcast_convert_type(lax.bitcast_convert_type(z, jnp.uint32) | jnp.uint32(SIGN_BIT), F32)
            sp = jnp.maximum(z, 0.0) + jnp.log2(1.0 + jnp.exp2(neg_abs))
            if strict is not None:
                sp = jnp.where(strict, sp, 0.0)
            hi32 = lax.bitcast_convert_type(lax.bitcast_convert_type(sp, jnp.uint32) & jnp.uint32(BF16_BITS), F32)
            hi, lo = hi32.astype(BF16), (sp - hi32).astype(BF16)
            rs.append(jnp.dot(jnp.concatenate([hi, lo], axis=1), u2, preferred_element_type=F32))
        return rs

    def accumulate(kb, rows, zs, rs, strict):
        start = pl.multiple_of(kb * tk, tk)
        for c, z, r in zip(cols, zs, rs):
            carry = carry_ref[rows, c]
            att = jnp.exp2(z + r[:, :tk] + carry)
            if strict is not None:
                att = jnp.where(strict, att, 0.0)
            acc_ref[rows, c] += jnp.dot(att.astype(BF16), v_ref[pl.ds(start, tk), c],
                                        preferred_element_type=F32)
            carry_ref[rows, c] = carry + r[:, tk:]

    for j in reversed(range(n_diag)):
        kb = qi * n_diag + j
        rows = slice(j * tk, tq)
        q_pos = qi * tq + j * tk + lax.broadcasted_iota(jnp.int32, (tq - j * tk, tk), 0)
        k_pos = kb * tk + lax.broadcasted_iota(jnp.int32, (tq - j * tk, tk), 1)
        strict = k_pos < q_pos
        zs = logits(kb, rows)
        accumulate(kb, rows, zs, cum_log_keep(zs, strict), strict)

    n_full = qi * n_diag
    full = slice(0, tq)
    for c, z in zip(cols, logits(jnp.maximum(n_full - 1, 0), full)):
        z_ref[:, c] = z

    def body(i, carry_unused):
        kb = n_full - 1 - i
        zs = [z_ref[:, c] for c in cols]
        rs = cum_log_keep(zs, None)
        z_next = logits(jnp.maximum(kb - 1, 0), full)
        accumulate(kb, full, zs, rs, None)
        for c, z in zip(cols, z_next):
            z_ref[:, c] = z
        return carry_unused

    lax.fori_loop(0, n_full, body, 0)
    o_ref[...] = acc_ref[...].astype(o_ref.dtype)


def sb_attention(qkv, *, batch, seq, heads, tq, tk, hb):
    dh = qkv.shape[1] // (3 * heads)
    assert dh == tk == LANES and seq % tq == 0 and tq % tk == 0 and heads % hb == 0
    qkv3 = qkv.reshape(batch, seq, 3 * heads * dh)
    row = lax.broadcasted_iota(jnp.int32, (tk, 2 * tk), 0)
    col = lax.broadcasted_iota(jnp.int32, (tk, 2 * tk), 1)
    u = -((row >= col) | (col >= tk)).astype(BF16)
    u2 = jnp.concatenate([u, u], axis=0)
    nhb = heads // hb
    out = pl.pallas_call(
        functools.partial(_sb_attn_kernel, tq=tq, tk=tk, dh=dh, hb=hb, scale=dh ** -0.5),
        grid=(batch, nhb, seq // tq),
        in_specs=[
            pl.BlockSpec((None, tq, hb * dh), lambda b, h, i: (b, i, h)),
            pl.BlockSpec((None, seq, hb * dh), lambda b, h, i: (b, 0, nhb + h)),
            pl.BlockSpec((None, seq, hb * dh), lambda b, h, i: (b, 0, 2 * nhb + h)),
            pl.BlockSpec((2 * tk, 2 * tk), lambda b, h, i: (0, 0)),
        ],
        out_specs=pl.BlockSpec((None, tq, hb * dh), lambda b, h, i: (b, i, h)),
        out_shape=jax.ShapeDtypeStruct((batch, seq, heads * dh), BF16),
        scratch_shapes=[pltpu.VMEM((tq, hb * dh), F32), pltpu.VMEM((tq, hb * tk), F32),
                        pltpu.VMEM((tq, hb * tk), F32)],
        compiler_params=_params(("parallel", "parallel", "arbitrary")),
        name="sb_attn",
    )(qkv3, qkv3, qkv3, u2)
    return out.reshape(batch * seq, heads * dh)


def _ssd_kernel(z_ref, x_ref, bc_ref, dt_ref, cwx_ref, cbx_ref, cwbc_ref, cbbc_ref, dtb_ref, alog_ref,
                dskip_ref, ng_ref, y_ref, xtail_ref, bctail_ref, state_ref, yacc_ref, *, chunk, groups, hpg, p, n):
    c = pl.program_id(1)
    L = chunk
    S = SUBLANES
    K = SSD_CONV

    @pl.when(c == 0)
    def _():
        xtail_ref[...] = jnp.zeros_like(xtail_ref)
        bctail_ref[...] = jnp.zeros_like(bctail_ref)
        state_ref[...] = jnp.zeros_like(state_ref)

    def conv_silu(cur_ref, tail_ref, w_ref, b_ref):
        cur = cur_ref[...]
        first_row = lax.broadcasted_iota(jnp.int32, (S, cur.shape[1]), 0) == 0
        acc = w_ref[0:1, :] * cur
        for k in range(1, K):
            last = acc[L - 1:L, :]
            shifted = pltpu.roll(acc, 1, 0)
            head = jnp.where(first_row, tail_ref[k - 1:k, :], shifted[:S, :])
            tail_ref[k - 1:k, :] = last
            acc = jnp.concatenate([head, shifted[S:, :]], axis=0) + w_ref[k:k + 1, :] * cur
        return _silu(acc + b_ref[...])

    xs = conv_silu(x_ref, xtail_ref, cwx_ref, cbx_ref)
    bc = conv_silu(bc_ref, bctail_ref, cwbc_ref, cbbc_ref)

    dt = _softplus(dt_ref[...] + dtb_ref[...])
    a = -jnp.exp(alog_ref[...]) * LOG2_E
    row = lax.broadcasted_iota(jnp.int32, (L, L), 0)
    col = lax.broadcasted_iota(jnp.int32, (L, L), 1)
    causal = col <= row
    tri = causal.astype(BF16)
    a_cum = sum(jnp.dot(tri, part, preferred_element_type=F32) for part in _split_bf16(dt * a, 3))
    a_cum_t = a_cum.T
    dt_t = dt.T
    e_cum = jnp.exp2(a_cum)
    w_end = jnp.exp2(a_cum[L - 1:L, :] - a_cum) * dt
    lane_lo = lax.broadcasted_iota(jnp.int32, (L, 2 * p), 1) < p
    lane_lo_n = lax.broadcasted_iota(jnp.int32, (n, 2 * p), 1) < p

    for g in range(groups):
        bm = bc[:, g * n:(g + 1) * n]
        cm = bc[:, (groups + g) * n:(groups + g + 1) * n].astype(BF16)
        cb = lax.dot_general(cm, bm.astype(BF16), (((1,), (1,)), ((), ())), preferred_element_type=F32)
        bm_t = bm.T.astype(BF16)
        for pair in range(hpg // 2):
            h0 = g * hpg + 2 * pair
            lanes = slice(h0 * p, (h0 + 2) * p)
            x2 = xs[:, lanes]
            x2b = x2.astype(BF16)
            ys = []
            for h in (h0, h0 + 1):
                seg = a_cum[:, h:h + 1] - a_cum_t[h:h + 1, :]
                m = cb * jnp.exp2(jnp.where(causal, seg, -jnp.inf)) * dt_t[h:h + 1, :]
                ys.append(jnp.dot(m.astype(BF16), x2b, preferred_element_type=F32))
            y_diag = jnp.where(lane_lo, ys[0], ys[1])
            prev = state_ref[:, lanes]
            e2 = jnp.where(lane_lo, e_cum[:, h0:h0 + 1], e_cum[:, h0 + 1:h0 + 2])
            y_off = jnp.dot(cm, prev.astype(BF16), preferred_element_type=F32) * e2
            w2 = jnp.where(lane_lo, w_end[:, h0:h0 + 1], w_end[:, h0 + 1:h0 + 2])
            st = jnp.dot(bm_t, (x2 * w2).astype(BF16), preferred_element_type=F32)
            dec = jnp.where(lane_lo_n, e_cum[L - 1:L, h0:h0 + 1], e_cum[L - 1:L, h0 + 1:h0 + 2])
            state_ref[:, lanes] = dec * prev + st
            yacc_ref[:, lanes] = y_diag + y_off + x2 * dskip_ref[:, lanes]

    gw = hpg * p
    for g in range(groups):
        lanes = slice(g * gw, (g + 1) * gw)
        yg = yacc_ref[:, lanes] * _silu(z_ref[:, lanes])
        ms = jnp.mean(yg * yg, axis=-1, keepdims=True)
        y_ref[:, lanes] = (yg * lax.rsqrt(ms + RMS_EPS) * ng_ref[:, lanes]).astype(y_ref.dtype)


def ssd_scan(proj, dt_raw, conv_w, conv_b, dt_bias, a_log, d_skip, norm_g, *, batch, seq, chunk, d_inner):
    groups, p, n = SSD_GROUPS, SSD_HEAD_DIM, SSD_STATE
    heads = d_inner // p
    hpg = heads // groups
    bcw = 2 * groups * n
    assert chunk == LANES and n == LANES and 2 * p == LANES and heads <= LANES and hpg % 2 == 0
    assert seq % chunk == 0 and d_inner % bcw == 0
    nc = seq // chunk
    pad = LANES - heads
    row2 = lambda v: v.reshape(1, -1)
    cwx, cwbc = conv_w[:, :d_inner], conv_w[:, d_inner:]
    cbx, cbbc = row2(conv_b[:d_inner]), row2(conv_b[d_inner:])
    dtb = row2(jnp.pad(dt_bias, (0, pad)))
    alog = row2(jnp.pad(a_log, (0, pad)))
    dskip = row2(jnp.repeat(d_skip, p))
    ng = row2(norm_g)
    x_blk = d_inner // d_inner
    bc_blk = 2 * d_inner // bcw
    tok = lambda b, c: b * nc + c
    const = lambda shape: pl.BlockSpec(shape, lambda b, c: (0, 0))
    return pl.pallas_call(
        functools.partial(_ssd_kernel, chunk=chunk, groups=groups, hpg=hpg, p=p, n=n),
        grid=(batch, nc),
        in_specs=[
            pl.BlockSpec((chunk, d_inner), lambda b, c: (tok(b, c), 0)),
            pl.BlockSpec((chunk, d_inner), lambda b, c: (tok(b, c), x_blk)),
            pl.BlockSpec((chunk, bcw), lambda b, c: (tok(b, c), bc_blk)),
            pl.BlockSpec((chunk, LANES), lambda b, c: (tok(b, c), 0)),
            const((SSD_CONV, d_inner)), const((1, d_inner)),
            const((SSD_CONV, bcw)), const((1, bcw)),
            const((1, LANES)), const((1, LANES)),
            const((1, d_inner)), const((1, d_inner)),
        ],
        out_specs=pl.BlockSpec((chunk, d_inner), lambda b, c: (tok(b, c), 0)),
        out_shape=jax.ShapeDtypeStruct((batch * seq, d_inner), BF16),
        scratch_shapes=[
            pltpu.VMEM((SUBLANES, d_inner), F32),
            pltpu.VMEM((SUBLANES, bcw), F32),
            pltpu.VMEM((n, d_inner), F32),
            pltpu.VMEM((chunk, d_inner), F32),
        ],
        compiler_params=_params(("parallel", "arbitrary")),
        name="ssd_scan",
    )(proj, proj, proj, dt_raw, cwx, cbx, cwbc, cbbc, dtb, alog, dskip, ng)


FFN_TM, FFN_TF, FFN_TS, FFN_TL = 1024, 512, 512, 256
PROJ_TM, PROJ_TN = 1024, 1024
OUT_TM, OUT_TS = 512, 128
SB_TQ, SB_TK, SB_HB = 256, 128, 8
SSD_CHUNK = 128
SSD_PROJ_TN = 2048


def kernel(x, ffn_w_gate_up, ffn_w_down, ln_g, ln_b, sb_w_in, sb_w_out, ssd_w_in, ssd_conv_w, ssd_conv_b,
           ssd_dt_bias, ssd_a_log, ssd_d, ssd_norm_g, ssd_w_out):
    batch, seq, d = x.shape
    m = batch * seq
    depth = ffn_w_gate_up.shape[0]
    assert depth == DEPTH
    xf = x.reshape(m, d)
    xb = xf.astype(BF16)
    row2 = lambda v: v.reshape(1, -1)

    d_inner = ssd_w_out.shape[1]
    n_main = 2 * d_inner + 2 * SSD_GROUPS * SSD_STATE
    nm, nf = m // FFN_TM, ffn_w_down.shape[2] // FFN_TF
    ffn_order = [(i, s) for i in range(depth) for s in range(2)]
    cast_weights = {ffn_order[0]: (ffn_w_gate_up[0, 0].astype(BF16), ffn_w_down[0, 0].astype(BF16))}

    def ffn(xb, xf, i, s):
        k = ffn_order.index((i, s))
        w_gate_up, w_down = cast_weights.pop((i, s))
        nxt = ffn_order[k + 1] if k + 1 < len(ffn_order) else None
        ssd_next = s == 1 and i + 1 < depth and (i + 1) % 2 == 1

        def by_rows(w, lead, cols=None):
            width = w.shape[-1] if cols is None else cols
            n_col_blocks = max(c for c in range(1, nf + 1) if (width // LANES) % c == 0)
            return cast_job(w, lead, nm, n_col_blocks, cols=cols)

        jobs, keys = [], []
        if nxt is not None:
            jobs += [cast_job(ffn_w_gate_up, nxt, nm, nf), cast_job(ffn_w_down, nxt, nf, nm, col_major_grid=True)]
            keys += [("gate_up",) + nxt, ("down",) + nxt]
        if ssd_next:
            jobs += [by_rows(ssd_w_in, ((i + 1) // 2,), cols=n_main)]
            keys += [("mix_in", i + 1)]
        if s == 0 and i % 2 == 0:
            jobs += [by_rows(sb_w_in, (i // 2,)), by_rows(sb_w_out, (i // 2,))]
            keys += [("mix_in", i), ("mix_out", i)]
        if s == 0 and i % 2 == 1:
            jobs += [by_rows(ssd_w_out, (i // 2,))]
            keys += [("mix_out", i)]
        yf, yb, *cast = ffn_ln(xb, xf, w_gate_up, w_down, row2(ln_g[i, 2 * s]), row2(ln_b[i, 2 * s]), jobs,
                               tm=FFN_TM, tf=FFN_TF, ts=FFN_TS, tl=FFN_TL)
        cast_weights.update(zip(keys, cast))
        if nxt is not None:
            cast_weights[nxt] = (cast_weights.pop(("gate_up",) + nxt), cast_weights.pop(("down",) + nxt))
        return yf, yb

    for i in range(depth):
        j = i // 2
        xf, xb = ffn(xb, xf, i, 0)
        if i % 2 == 0:
            qkv = matmul(xb, cast_weights.pop(("mix_in", i)), BF16, tm=PROJ_TM, tn=PROJ_TN)
            mix = sb_attention(qkv, batch=batch, seq=seq, heads=SB_HEADS, tq=SB_TQ, tk=SB_TK, hb=SB_HB)
        else:
            proj = matmul(xb, cast_weights.pop(("mix_in", i)), F32, tm=PROJ_TM, tn=SSD_PROJ_TN)
            w_dt = jnp.pad(ssd_w_in[j, :, n_main:], ((0, 0), (0, LANES - ssd_dt_bias.shape[1]))).astype(BF16)
            dt_raw = matmul(xb, w_dt, F32, tm=PROJ_TM, tn=LANES)
            mix = ssd_scan(proj, dt_raw, ssd_conv_w[j], ssd_conv_b[j],
                           ssd_dt_bias[j], ssd_a_log[j], ssd_d[j], ssd_norm_g[j],
                           batch=batch, seq=seq, chunk=SSD_CHUNK, d_inner=d_inner)
        xf, xb = proj_ln(mix, cast_weights.pop(("mix_out", i)), xf, row2(ln_g[i, 1]), row2(ln_b[i, 1]),
                         tm=OUT_TM, ts=OUT_TS)
        xf, xb = ffn(xb, xf, i, 1)
    return xf.reshape(batch, seq, d)
```

```python
import functools

import jax
import jax.numpy as jnp
from jax import lax
from jax.experimental import pallas as pl
from jax.experimental.pallas import tpu as pltpu

F32 = jnp.float32
BF16 = jnp.bfloat16

LOG2_E = 1.4426950408889634
SIGN_BIT = 0x80000000
BF16_BITS = 0xFFFF0000
LN_EPS = 1e-5
RMS_EPS = 1e-5
DEPTH = 2
DEEPNORM_ALPHA = (2.0 * DEPTH) ** 0.25

SB_HEADS = 16
SSD_HEAD_DIM = 64
SSD_GROUPS = 8
SSD_STATE = 128
SSD_CONV = 4

LANES = 128
SUBLANES = 8
VMEM_LIMIT = 56 * 1024 * 1024


def _params(sem):
    return pltpu.CompilerParams(dimension_semantics=sem, vmem_limit_bytes=VMEM_LIMIT)


def _layer_norm(r, g, b):
    mu = jnp.mean(r, axis=-1, keepdims=True)
    c = r - mu
    var = jnp.mean(c * c, axis=-1, keepdims=True)
    return c * lax.rsqrt(var + LN_EPS) * g + b


def _softplus(x):
    return jnp.maximum(x, 0.0) + jnp.log1p(jnp.exp(-jnp.abs(x)))


def _silu(x):
    return x * jax.nn.sigmoid(x)


def _split_bf16(x, parts):
    out = []
    for _ in range(parts - 1):
        p = x.astype(BF16)
        out.append(p)
        x = x - p.astype(F32)
    out.append(x.astype(BF16))
    return out


def _ffn_ln_kernel(*refs, nm, nf, ts, tl, n_casts):
    xb_ref, xf_hbm, wg_ref, wu_ref, wd_ref, g_ref, b_ref = refs[:7]
    cast_in = refs[7:7 + n_casts]
    yf_hbm, yb_ref = refs[7 + n_casts:9 + n_casts]
    cast_out = refs[9 + n_casts:9 + 2 * n_casts]
    acc_ref, res_ref, sem_in, sem_out = refs[9 + 2 * n_casts:]
    for src, dst in zip(cast_in, cast_out):
        dst[...] = src[...].astype(BF16)
    i, j = pl.program_id(0), pl.program_id(1)
    tm = acc_ref.shape[0]

    def tile_rows(t):
        return pl.ds(pl.multiple_of(t * tm, tm), tm)

    def fetch_residual():
        return pltpu.make_async_copy(xf_hbm.at[tile_rows(i), :], res_ref, sem_in)

    def write_back(t):
        return pltpu.make_async_copy(res_ref, yf_hbm.at[tile_rows(t), :], sem_out)

    @pl.when(j == 0)
    def _():
        acc_ref[...] = jnp.zeros_like(acc_ref)

    @pl.when(j == nf - 2)
    def _():
        @pl.when(i > 0)
        def _():
            write_back(i - 1).wait()

        fetch_residual().start()

    for r0 in range(0, tm, ts):
        rows = slice(r0, r0 + ts)
        xb = xb_ref[rows, :]
        gate = jnp.dot(xb, wg_ref[...], preferred_element_type=F32)
        up = jnp.dot(xb, wu_ref[...], preferred_element_type=F32)
        h = (_silu(gate) * up).astype(BF16)
        acc_ref[rows, :] += jnp.dot(h, wd_ref[...], preferred_element_type=F32)

    @pl.when(j == nf - 1)
    def _():
        fetch_residual().wait()
        for r0 in range(0, tm, tl):
            rows = slice(r0, r0 + tl)
            y = _layer_norm(DEEPNORM_ALPHA * res_ref[rows, :] + 0.5 * acc_ref[rows, :], g_ref[...], b_ref[...])
            res_ref[rows, :] = y
            yb_ref[rows, :] = y.astype(BF16)
        write_back(i).start()

        @pl.when(i == nm - 1)
        def _():
            write_back(i).wait()


def cast_job(w, lead, n_row_blocks, n_col_blocks, cols=None, steps_per_row_block=None):
    rows, width = w.shape[len(lead):]
    cols = width if cols is None else cols
    assert rows % n_row_blocks == 0 and cols % n_col_blocks == 0
    blk = (rows // n_row_blocks, cols // n_col_blocks)
    assert blk[0] % (2 * SUBLANES) == 0 and blk[1] % LANES == 0

    def block_index(i, j):
        r, c = (i, j) if steps_per_row_block is None else (i * steps_per_row_block + j, 0)
        return jnp.minimum(r, n_row_blocks - 1), jnp.minimum(c, n_col_blocks - 1)

    in_spec = pl.BlockSpec((None,) * len(lead) + blk, lambda i, j: tuple(lead) + block_index(i, j))
    out_spec = pl.BlockSpec(blk, block_index)
    return w, in_spec, out_spec, jax.ShapeDtypeStruct((rows, cols), BF16)


def ffn_ln(xb, xf, w_gate_up, w_down, g, b, casts, *, tm, tf, ts, tl):
    m, d = xf.shape
    f = w_down.shape[0]
    nm, nf = m // tm, f // tf
    assert m % tm == 0 and f % tf == 0 and nf >= 2 and tm % ts == 0 and tm % tl == 0
    assert w_gate_up.shape == (d, 2 * f)
    return pl.pallas_call(
        functools.partial(_ffn_ln_kernel, nm=nm, nf=nf, ts=ts, tl=tl, n_casts=len(casts)),
        grid=(nm, nf),
        in_specs=[
            pl.BlockSpec((tm, d), lambda i, j: (i, 0)),
            pl.BlockSpec(memory_space=pl.ANY),
            pl.BlockSpec((d, tf), lambda i, j: (0, j)),
            pl.BlockSpec((d, tf), lambda i, j: (0, j + nf)),
            pl.BlockSpec((tf, d), lambda i, j: (j, 0)),
            pl.BlockSpec((1, d), lambda i, j: (0, 0)),
            pl.BlockSpec((1, d), lambda i, j: (0, 0)),
        ] + [c[1] for c in casts],
        out_specs=[
            pl.BlockSpec(memory_space=pl.ANY),
            pl.BlockSpec((tm, d), lambda i, j: (i, 0)),
        ] + [c[2] for c in casts],
        out_shape=[jax.ShapeDtypeStruct((m, d), F32), jax.ShapeDtypeStruct((m, d), BF16)] + [c[3] for c in casts],
        scratch_shapes=[pltpu.VMEM((tm, d), F32), pltpu.VMEM((tm, d), F32),
                        pltpu.SemaphoreType.DMA(()), pltpu.SemaphoreType.DMA(())],
        compiler_params=_params(("arbitrary", "arbitrary")),
        name="ffn_ln",
    )(xb, xf, w_gate_up, w_gate_up, w_down, g, b, *[c[0] for c in casts])


def _matmul_kernel(x_ref, w_ref, o_ref):
    o_ref[...] = jnp.dot(x_ref[...], w_ref[...], preferred_element_type=F32).astype(o_ref.dtype)


def matmul(x, w, out_dtype, *, tm, tn):
    m, k = x.shape
    n = w.shape[1]
    assert m % tm == 0 and n % tn == 0
    return pl.pallas_call(
        _matmul_kernel,
        grid=(n // tn, m // tm),
        in_specs=[
            pl.BlockSpec((tm, k), lambda j, i: (i, 0)),
            pl.BlockSpec((k, tn), lambda j, i: (0, j)),
        ],
        out_specs=pl.BlockSpec((tm, tn), lambda j, i: (i, j)),
        out_shape=jax.ShapeDtypeStruct((m, n), out_dtype),
        compiler_params=_params(("parallel", "parallel")),
        name="in_proj",
    )(x, w)


def _proj_ln_kernel(a_ref, w_ref, xf_ref, g_ref, b_ref, yf_ref, yb_ref, *, ts):
    for r0 in range(0, a_ref.shape[0], ts):
        rows = slice(r0, r0 + ts)
        mix = jnp.dot(a_ref[rows, :], w_ref[...], preferred_element_type=F32)
        y = _layer_norm(DEEPNORM_ALPHA * xf_ref[rows, :] + mix, g_ref[...], b_ref[...])
        yf_ref[rows, :] = y
        yb_ref[rows, :] = y.astype(BF16)


def proj_ln(a, w, xf, g, b, *, tm, ts):
    m, kdim = a.shape
    d = w.shape[1]
    assert m % tm == 0 and tm % ts == 0
    return pl.pallas_call(
        functools.partial(_proj_ln_kernel, ts=ts),
        grid=(m // tm,),
        in_specs=[
            pl.BlockSpec((tm, kdim), lambda i: (i, 0)),
            pl.BlockSpec((kdim, d), lambda i: (0, 0), pipeline_mode=pl.Buffered(1)),
            pl.BlockSpec((tm, d), lambda i: (i, 0)),
            pl.BlockSpec((1, d), lambda i: (0, 0)),
            pl.BlockSpec((1, d), lambda i: (0, 0)),
        ],
        out_specs=[
            pl.BlockSpec((tm, d), lambda i: (i, 0)),
            pl.BlockSpec((tm, d), lambda i: (i, 0)),
        ],
        out_shape=[jax.ShapeDtypeStruct((m, d), F32), jax.ShapeDtypeStruct((m, d), BF16)],
        compiler_params=_params(("parallel",)),
        name="proj_ln",
    )(a, w, xf, g, b)


def _sb_attn_kernel(q_ref, k_ref, v_ref, u_ref, o_ref, acc_ref, carry_ref, z_ref, *, tq, tk, dh, hb, scale):
    qi = pl.program_id(2)
    u2 = u_ref[...]
    acc_ref[...] = jnp.zeros_like(acc_ref)
    carry_ref[...] = jnp.zeros_like(carry_ref)
    n_diag = tq // tk
    cols = [slice(h * dh, (h + 1) * dh) for h in range(hb)]
    to_log2 = scale * LOG2_E

    def logits(kb, rows):
        start = pl.multiple_of(kb * tk, tk)
        return [lax.dot_general(q_ref[rows, c], k_ref[pl.ds(start, tk), c], (((1,), (1,)), ((), ())),
                                preferred_element_type=F32) * to_log2 for c in cols]

    def cum_log_keep(zs, strict):
        rs = []
        for z in zs:
            neg_abs = lax.bitcast_convert_type(lax.bitcast_convert_type(z, jnp.uint32) | jnp.uint32(SIGN_BIT), F32)
            sp = jnp.maximum(z, 0.0) + jnp.log2(1.0 + jnp.exp2(neg_abs))
            if strict is not None:
                sp = jnp.where(strict, sp, 0.0)
            hi32 = lax.bitcast_convert_type(lax.bitcast_convert_type(sp, jnp.uint32) & jnp.uint32(BF16_BITS), F32)
            hi, lo = hi32.astype(BF16), (sp - hi32).astype(BF16)
            rs.append(jnp.dot(jnp.concatenate([hi, lo], axis=1), u2, preferred_element_type=F32))
        return rs

    def accumulate(kb, rows, zs, rs, strict):
        start = pl.multiple_of(kb * tk, tk)
        for c, z, r in zip(cols, zs, rs):
            carry = carry_ref[rows, c]
            att = jnp.exp2(z + r[:, :tk] + carry)
            if strict is not None:
                att = jnp.where(strict, att, 0.0)
            acc_ref[rows, c] += jnp.dot(att.astype(BF16), v_ref[pl.ds(start, tk), c],
                                        preferred_element_type=F32)
            carry_ref[rows, c] = carry + r[:, tk:]

    for j in reversed(range(n_diag)):
        kb = qi * n_diag + j
        rows = slice(j * tk, tq)
        q_pos = qi * tq + j * tk + lax.broadcasted_iota(jnp.int32, (tq - j * tk, tk), 0)
        k_pos = kb * tk + lax.broadcasted_iota(jnp.int32, (tq - j * tk, tk), 1)
        strict = k_pos < q_pos
        zs = logits(kb, rows)
        accumulate(kb, rows, zs, cum_log_keep(zs, strict), strict)

    n_full = qi * n_diag
    full = slice(0, tq)
    for c, z in zip(cols, logits(jnp.maximum(n_full - 1, 0), full)):
        z_ref[:, c] = z

    def body(i, carry_unused):
        kb = n_full - 1 - i
        zs = [z_ref[:, c] for c in cols]
        rs = cum_log_keep(zs, None)
        z_next = logits(jnp.maximum(kb - 1, 0), full)
        accumulate(kb, full, zs, rs, None)
        for c, z in zip(cols, z_next):
            z_ref[:, c] = z
        return carry_unused

    lax.fori_loop(0, n_full, body, 0)
    o_ref[...] = acc_ref[...].astype(o_ref.dtype)


def sb_attention(qkv, *, batch, seq, heads, tq, tk, hb):
    dh = qkv.shape[1] // (3 * heads)
    assert dh == tk == LANES and seq % tq == 0 and tq % tk == 0 and heads % hb == 0
    qkv3 = qkv.reshape(batch, seq, 3 * heads * dh)
    row = lax.broadcasted_iota(jnp.int32, (tk, 2 * tk), 0)
    col = lax.broadcasted_iota(jnp.int32, (tk, 2 * tk), 1)
    u = -((row >= col) | (col >= tk)).astype(BF16)
    u2 = jnp.concatenate([u, u], axis=0)
    nhb = heads // hb
    out = pl.pallas_call(
        functools.partial(_sb_attn_kernel, tq=tq, tk=tk, dh=dh, hb=hb, scale=dh ** -0.5),
        grid=(batch, nhb, seq // tq),
        in_specs=[
            pl.BlockSpec((None, tq, hb * dh), lambda b, h, i: (b, i, h)),
            pl.BlockSpec((None, seq, hb * dh), lambda b, h, i: (b, 0, nhb + h)),
            pl.BlockSpec((None, seq, hb * dh), lambda b, h, i: (b, 0, 2 * nhb + h)),
            pl.BlockSpec((2 * tk, 2 * tk), lambda b, h, i: (0, 0)),
        ],
        out_specs=pl.BlockSpec((None, tq, hb * dh), lambda b, h, i: (b, i, h)),
        out_shape=jax.ShapeDtypeStruct((batch, seq, heads * dh), BF16),
        scratch_shapes=[pltpu.VMEM((tq, hb * dh), F32), pltpu.VMEM((tq, hb * tk), F32),
                        pltpu.VMEM((tq, hb * tk), F32)],
        compiler_params=_params(("parallel", "parallel", "arbitrary")),
        name="sb_attn",
    )(qkv3, qkv3, qkv3, u2)
    return out.reshape(batch * seq, heads * dh)


def _ssd_kernel(z_ref, x_ref, bc_ref, dt_ref, cwx_ref, cbx_ref, cwbc_ref, cbbc_ref, dtb_ref, alog_ref,
                dskip_ref, ng_ref, y_ref, xtail_ref, bctail_ref, state_ref, yacc_ref, *, chunk, groups, hpg, p, n):
    c = pl.program_id(1)
    L = chunk
    S = SUBLANES
    K = SSD_CONV

    @pl.when(c == 0)
    def _():
        xtail_ref[...] = jnp.zeros_like(xtail_ref)
        bctail_ref[...] = jnp.zeros_like(bctail_ref)
        state_ref[...] = jnp.zeros_like(state_ref)

    def conv_silu(cur_ref, tail_ref, w_ref, b_ref):
        cur = cur_ref[...]
        first_row = lax.broadcasted_iota(jnp.int32, (S, cur.shape[1]), 0) == 0
        acc = w_ref[0:1, :] * cur
        for k in range(1, K):
            last = acc[L - 1:L, :]
            shifted = pltpu.roll(acc, 1, 0)
            head = jnp.where(first_row, tail_ref[k - 1:k, :], shifted[:S, :])
            tail_ref[k - 1:k, :] = last
            acc = jnp.concatenate([head, shifted[S:, :]], axis=0) + w_ref[k:k + 1, :] * cur
        return _silu(acc + b_ref[...])

    xs = conv_silu(x_ref, xtail_ref, cwx_ref, cbx_ref)
    bc = conv_silu(bc_ref, bctail_ref, cwbc_ref, cbbc_ref)

    dt = _softplus(dt_ref[...] + dtb_ref[...])
    a = -jnp.exp(alog_ref[...]) * LOG2_E
    row = lax.broadcasted_iota(jnp.int32, (L, L), 0)
    col = lax.broadcasted_iota(jnp.int32, (L, L), 1)
    causal = col <= row
    tri = causal.astype(BF16)
    a_cum = sum(jnp.dot(tri, part, preferred_element_type=F32) for part in _split_bf16(dt * a, 3))
    a_cum_t = a_cum.T
    dt_t = dt.T
    e_cum = jnp.exp2(a_cum)
    w_end = jnp.exp2(a_cum[L - 1:L, :] - a_cum) * dt
    lane_lo = lax.broadcasted_iota(jnp.int32, (L, 2 * p), 1) < p
    lane_lo_n = lax.broadcasted_iota(jnp.int32, (n, 2 * p), 1) < p

    for g in range(groups):
        bm = bc[:, g * n:(g + 1) * n]
        cm = bc[:, (groups + g) * n:(groups + g + 1) * n].astype(BF16)
        cb = lax.dot_general(cm, bm.astype(BF16), (((1,), (1,)), ((), ())), preferred_element_type=F32)
        bm_t = bm.T.astype(BF16)
        for pair in range(hpg // 2):
            h0 = g * hpg + 2 * pair
            lanes = slice(h0 * p, (h0 + 2) * p)
            x2 = xs[:, lanes]
            x2b = x2.astype(BF16)
            ys = []
            for h in (h0, h0 + 1):
                seg = a_cum[:, h:h + 1] - a_cum_t[h:h + 1, :]
                m = cb * jnp.exp2(jnp.where(causal, seg, -jnp.inf)) * dt_t[h:h + 1, :]
                ys.append(jnp.dot(m.astype(BF16), x2b, preferred_element_type=F32))
            y_diag = jnp.where(lane_lo, ys[0], ys[1])
            prev = state_ref[:, lanes]
            e2 = jnp.where(lane_lo, e_cum[:, h0:h0 + 1], e_cum[:, h0 + 1:h0 + 2])
            y_off = jnp.dot(cm, prev.astype(BF16), preferred_element_type=F32) * e2
            w2 = jnp.where(lane_lo, w_end[:, h0:h0 + 1], w_end[:, h0 + 1:h0 + 2])
            st = jnp.dot(bm_t, (x2 * w2).astype(BF16), preferred_element_type=F32)
            dec = jnp.where(lane_lo_n, e_cum[L - 1:L, h0:h0 + 1], e_cum[L - 1:L, h0 + 1:h0 + 2])
            state_ref[:, lanes] = dec * prev + st
            yacc_ref[:, lanes] = y_diag + y_off + x2 * dskip_ref[:, lanes]

    gw = hpg * p
    for g in range(groups):
        lanes = slice(g * gw, (g + 1) * gw)
        yg = yacc_ref[:, lanes] * _silu(z_ref[:, lanes])
        ms = jnp.mean(yg * yg, axis=-1, keepdims=True)
        y_ref[:, lanes] = (yg * lax.rsqrt(ms + RMS_EPS) * ng_ref[:, lanes]).astype(y_ref.dtype)


def ssd_scan(proj, dt_raw, conv_w, conv_b, dt_bias, a_log, d_skip, norm_g, *, batch, seq, chunk, d_inner):
    groups, p, n = SSD_GROUPS, SSD_HEAD_DIM, SSD_STATE
    heads = d_inner // p
    hpg = heads // groups
    bcw = 2 * groups * n
    assert chunk == LANES and n == LANES and 2 * p == LANES and heads <= LANES and hpg % 2 == 0
    assert seq % chunk == 0 and d_inner % bcw == 0
    nc = seq // chunk
    pad = LANES - heads
    row2 = lambda v: v.reshape(1, -1)
    cwx, cwbc = conv_w[:, :d_inner], conv_w[:, d_inner:]
    cbx, cbbc = row2(conv_b[:d_inner]), row2(conv_b[d_inner:])
    dtb = row2(jnp.pad(dt_bias, (0, pad)))
    alog = row2(jnp.pad(a_log, (0, pad)))
    dskip = row2(jnp.repeat(d_skip, p))
    ng = row2(norm_g)
    x_blk = d_inner // d_inner
    bc_blk = 2 * d_inner // bcw
    tok = lambda b, c: b * nc + c
    const = lambda shape: pl.BlockSpec(shape, lambda b, c: (0, 0))
    return pl.pallas_call(
        functools.partial(_ssd_kernel, chunk=chunk, groups=groups, hpg=hpg, p=p, n=n),
        grid=(batch, nc),
        in_specs=[
            pl.BlockSpec((chunk, d_inner), lambda b, c: (tok(b, c), 0)),
            pl.BlockSpec((chunk, d_inner), lambda b, c: (tok(b, c), x_blk)),
            pl.BlockSpec((chunk, bcw), lambda b, c: (tok(b, c), bc_blk)),
            pl.BlockSpec((chunk, LANES), lambda b, c: (tok(b, c), 0)),
            const((SSD_CONV, d_inner)), const((1, d_inner)),
            const((SSD_CONV, bcw)), const((1, bcw)),
            const((1, LANES)), const((1, LANES)),
            const((1, d_inner)), const((1, d_inner)),
        ],
        out_specs=pl.BlockSpec((chunk, d_inner), lambda b, c: (tok(b, c), 0)),
        out_shape=jax.ShapeDtypeStruct((batch * seq, d_inner), BF16),
        scratch_shapes=[
            pltpu.VMEM((SUBLANES, d_inner), F32),
            pltpu.VMEM((SUBLANES, bcw), F32),
            pltpu.VMEM((n, d_inner), F32),
            pltpu.VMEM((chunk, d_inner), F32),
        ],
        compiler_params=_params(("parallel", "arbitrary")),
        name="ssd_scan",
    )(proj, proj, proj, dt_raw, cwx, cbx, cwbc, cbbc, dtb, alog, dskip, ng)


FFN_TM, FFN_TF, FFN_TS, FFN_TL = 1024, 512, 512, 256
PROJ_TM, PROJ_TN = 1024, 1024
SB_PROJ_TM = 2048
OUT_TM, OUT_TS = 512, 128
SB_TQ, SB_TK, SB_HB = 256, 128, 8
SSD_CHUNK = 128
SSD_PROJ_TN = 2048


def kernel(x, ffn_w_gate_up, ffn_w_down, ln_g, ln_b, sb_w_in, sb_w_out, ssd_w_in, ssd_conv_w, ssd_conv_b,
           ssd_dt_bias, ssd_a_log, ssd_d, ssd_norm_g, ssd_w_out):
    batch, seq, d = x.shape
    m = batch * seq
    depth = ffn_w_gate_up.shape[0]
    assert depth == DEPTH
    xf = x.reshape(m, d)
    xb = xf.astype(BF16)
    row2 = lambda v: v.reshape(1, -1)

    d_inner = ssd_w_out.shape[1]
    n_main = 2 * d_inner + 2 * SSD_GROUPS * SSD_STATE
    nm, nf = m // FFN_TM, ffn_w_down.shape[2] // FFN_TF
    ffn_order = [(i, s) for i in range(depth) for s in range(2)]
    cast_weights = {ffn_order[0]: (ffn_w_gate_up[0, 0].astype(BF16), ffn_w_down[0, 0].astype(BF16))}

    def ffn(xb, xf, i, s):
        k = ffn_order.index((i, s))
        w_gate_up, w_down = cast_weights.pop((i, s))
        nxt = ffn_order[k + 1] if k + 1 < len(ffn_order) else None
        ssd_next = s == 1 and i + 1 < depth and (i + 1) % 2 == 1

        def by_rows(w, lead, cols=None):
            width = w.shape[-1] if cols is None else cols
            n_col_blocks = max(c for c in range(1, nf + 1) if (width // LANES) % c == 0)
            return cast_job(w, lead, nm, n_col_blocks, cols=cols)

        jobs, keys = [], []
        if nxt is not None:
            jobs += [cast_job(ffn_w_gate_up, nxt, nm, nf),
                     cast_job(ffn_w_down, nxt, nm * nf, 1, steps_per_row_block=nf)]
            keys += [("gate_up",) + nxt, ("down",) + nxt]
        if ssd_next:
            jobs += [by_rows(ssd_w_in, ((i + 1) // 2,), cols=n_main)]
            keys += [("mix_in", i + 1)]
        if s == 0 and i % 2 == 0:
            jobs += [by_rows(sb_w_in, (i // 2,)), by_rows(sb_w_out, (i // 2,))]
            keys += [("mix_in", i), ("mix_out", i)]
        if s == 0 and i % 2 == 1:
            jobs += [by_rows(ssd_w_out, (i // 2,))]
            keys += [("mix_out", i)]
        yf, yb, *cast = ffn_ln(xb, xf, w_gate_up, w_down, row2(ln_g[i, 2 * s]), row2(ln_b[i, 2 * s]), jobs,
                               tm=FFN_TM, tf=FFN_TF, ts=FFN_TS, tl=FFN_TL)
        cast_weights.update(zip(keys, cast))
        if nxt is not None:
            cast_weights[nxt] = (cast_weights.pop(("gate_up",) + nxt), cast_weights.pop(("down",) + nxt))
        return yf, yb

    for i in range(depth):
        j = i // 2
        xf, xb = ffn(xb, xf, i, 0)
        if i % 2 == 0:
            qkv = matmul(xb, cast_weights.pop(("mix_in", i)), BF16, tm=SB_PROJ_TM, tn=PROJ_TN)
            mix = sb_attention(qkv, batch=batch, seq=seq, heads=SB_HEADS, tq=SB_TQ, tk=SB_TK, hb=SB_HB)
        else:
            proj = matmul(xb, cast_weights.pop(("mix_in", i)), F32, tm=PROJ_TM, tn=SSD_PROJ_TN)
            w_dt = jnp.pad(ssd_w_in[j, :, n_main:], ((0, 0), (0, LANES - ssd_dt_bias.shape[1]))).astype(BF16)
            dt_raw = matmul(xb, w_dt, F32, tm=PROJ_TM, tn=LANES)
            mix = ssd_scan(proj, dt_raw, ssd_conv_w[j], ssd_conv_b[j],
                           ssd_dt_bias[j], ssd_a_log[j], ssd_d[j], ssd_norm_g[j],
                           batch=batch, seq=seq, chunk=SSD_CHUNK, d_inner=d_inner)
        xf, xb = proj_ln(mix, cast_weights.pop(("mix_out", i)), xf, row2(ln_g[i, 1]), row2(ln_b[i, 1]),
                         tm=OUT_TM, ts=OUT_TS)
        xf, xb = ffn(xb, xf, i, 1)
    return xf.reshape(batch, seq, d)
```
